```python
import math
import jax, jax.numpy as jnp
from jax import lax
import numpy as np

D_MODEL = 2048
BATCH = 2
SEQ = 16384
DEPTH = 2

GRID_W = 64
CTX_LEN = 256
EPS = 1e-6

ATTN_WIDTH = D_MODEL // 2
ATTN_V_DIM = 128
ATTN_HEADS = ATTN_WIDTH // ATTN_V_DIM
ATTN_QK_DIM = ATTN_V_DIM // 2
ROPE_AXIS_DIM = ATTN_QK_DIM // 2
ROPE_BASE = 10000.0
Q_BLOCK = 128

POOL_WINDOWS = (2, 4, 8, 16)
POOL_WIDTH = D_MODEL // 4
POOL_GROUP = POOL_WIDTH // len(POOL_WINDOWS)

GMLP_WIDTH = D_MODEL // 4
GMLP_GROUPS = 4
GMLP_GROUP = GMLP_WIDTH // GMLP_GROUPS
GMLP_CHUNK = 128

MIX_WIDTH = ATTN_WIDTH + POOL_WIDTH + GMLP_WIDTH
Q_OFF = 0
K_OFF = ATTN_WIDTH
V_OFF = 2 * ATTN_WIDTH
POOL_OFF = 3 * ATTN_WIDTH
GU_OFF = POOL_OFF + POOL_WIDTH
GV_OFF = GU_OFF + GMLP_WIDTH
IN_COLS = GV_OFF + GMLP_WIDTH

N_GROUPS = 4
EXPERTS_PER_GROUP = 8
N_EXPERTS = N_GROUPS * EXPERTS_PER_GROUP
TOP_K = 2
EXPERT_FF = D_MODEL // 4
MOE_BLOCK = 256

kernel_name = "hybrid_diffattn_pool_gmlp_hmoe_dit"


def rms_norm(x, g):
    xf = x.astype(jnp.float32)
    y = xf * lax.rsqrt(jnp.mean(xf * xf, axis=-1, keepdims=True) + EPS)
    return (y * g.astype(jnp.float32)).astype(x.dtype)


def layer_norm(x, g):
    xf = x.astype(jnp.float32)
    mu = jnp.mean(xf, axis=-1, keepdims=True)
    xc = xf - mu
    y = xc * lax.rsqrt(jnp.mean(xc * xc, axis=-1, keepdims=True) + EPS)
    return (y * g.astype(jnp.float32)).astype(x.dtype)


def axial_angles(n):
    rows = n // GRID_W
    row = jnp.repeat(jnp.arange(rows, dtype=jnp.float32), GRID_W)
    col = jnp.tile(jnp.arange(GRID_W, dtype=jnp.float32), rows)
    half = ROPE_AXIS_DIM // 2
    inv = ROPE_BASE ** (-jnp.arange(half, dtype=jnp.float32) / half)
    return row[:, None] * inv[None, :], col[:, None] * inv[None, :]


def rope_2d(x, ang_r, ang_c):
    half = ROPE_AXIS_DIM // 2

    def rot(seg, ang):
        cos = jnp.cos(ang)[None, :, None, None, :]
        sin = jnp.sin(ang)[None, :, None, None, :]
        a, b = seg[..., :half], seg[..., half:]
        return jnp.concatenate([a * cos - b * sin, b * cos + a * sin], axis=-1)

    xf = x.astype(jnp.float32)
    out = jnp.concatenate([rot(xf[..., :ROPE_AXIS_DIM], ang_r),
                           rot(xf[..., ROPE_AXIS_DIM:], ang_c)], axis=-1)
    return out.astype(x.dtype)


def split_qk(z):
    b, n, _ = z.shape
    return z.reshape(b, n, ATTN_HEADS, 2, ATTN_QK_DIM)


def split_v(z):
    b, n, _ = z.shape
    return z.reshape(b, n, ATTN_HEADS, ATTN_V_DIM)


def diff_attend(q, k, v, lam):
    s = jnp.einsum('bqhmd,bkhmd->bhmqk', q * (ATTN_QK_DIM ** -0.5), k).astype(jnp.float32)
    p = jax.nn.softmax(s, axis=-1)
    a = p[:, :, 0] - lam * p[:, :, 1]
    return jnp.einsum('bhqk,bkhe->bqhe', a.astype(v.dtype), v)


def diff_head_out(o, subln_g, lam_init):
    b, n = o.shape[:2]
    return (rms_norm(o, subln_g) * (1.0 - lam_init)).reshape(b, n, ATTN_WIDTH)


def multiscale_pool(p, pool_w, pool_scale):
    b, n, _ = p.shape
    pf = p.astype(jnp.float32)
    cs = jnp.concatenate([jnp.zeros((b, 1, POOL_WIDTH), jnp.float32),
                          jnp.cumsum(pf, axis=1)], axis=1)
    t = jnp.arange(n)
    outs = []
    for gi, win in enumerate(POOL_WINDOWS):
        lo = jnp.clip(t - win // 2, 0, n - 1)
        hi = jnp.clip(t + (win - 1 - win // 2), 0, n - 1)
        sl = slice(gi * POOL_GROUP, (gi + 1) * POOL_GROUP)
        csg = cs[:, :, sl]
        tot = jnp.take(csg, hi + 1, axis=1) - jnp.take(csg, lo, axis=1)
        cnt = (hi - lo + 1).astype(jnp.float32)[None, :, None]
        outs.append(tot / cnt - pf[:, :, sl])
    d = jnp.stack(outs, axis=2).astype(p.dtype)
    y = jnp.einsum('bngc,gcd->bngd', d, pool_w).reshape(b, n, POOL_WIDTH)
    return y * pool_scale


def chunk_gmlp(u, v, norm_g, ws, bs):
    b, n, _ = u.shape
    u = jax.nn.gelu(u)
    v = layer_norm(jax.nn.gelu(v), norm_g)
    vc = v.reshape(b, n // GMLP_CHUNK, GMLP_CHUNK, GMLP_GROUPS, GMLP_GROUP)
    sv = jnp.einsum('gij,bnjgc->bnigc', ws, vc) + bs.T[None, None, :, :, None]
    return u * sv.reshape(b, n, GMLP_WIDTH)


def local_mixers(proj, pool_w, pool_scale, gmlp_norm_g, gmlp_ws, gmlp_bs):
    y_pool = multiscale_pool(proj[..., POOL_OFF:GU_OFF], pool_w, pool_scale)
    y_gmlp = chunk_gmlp(proj[..., GU_OFF:GV_OFF], proj[..., GV_OFF:IN_COLS],
                        gmlp_norm_g, gmlp_ws, gmlp_bs)
    return y_pool, y_gmlp


def hier_moe(h, rg_w, rg_b, re_w, re_b, w_gate, w_up, w_down):
    T, d = h.shape
    gl = (h @ rg_w).astype(jnp.float32) + rg_b.astype(jnp.float32)
    pg = jax.nn.softmax(gl, axis=-1)
    g_val, g_idx = lax.top_k(pg, 1)
    el = ((h @ re_w).astype(jnp.float32) + re_b.astype(jnp.float32)).reshape(
        T, N_GROUPS, EXPERTS_PER_GROUP)
    el = jnp.take_along_axis(el, g_idx[:, :, None], axis=1)[:, 0]
    pe = jax.nn.softmax(el, axis=-1)
    e_val, e_idx = lax.top_k(pe, TOP_K)
    wts = g_val * e_val / jnp.sum(e_val, axis=-1, keepdims=True)
    eid = g_idx * EXPERTS_PER_GROUP + e_idx

    n_assign = T * TOP_K
    flat_e = eid.reshape(-1).astype(jnp.int32)
    flat_w = wts.reshape(-1)
    flat_t = jnp.repeat(jnp.arange(T, dtype=jnp.int32), TOP_K)
    order = jnp.argsort(flat_e)
    se, sw, st = flat_e[order], flat_w[order], flat_t[order]
    counts = jnp.bincount(flat_e, length=N_EXPERTS)
    start = jnp.cumsum(counts) - counts
    padded = (counts + MOE_BLOCK - 1) // MOE_BLOCK * MOE_BLOCK
    pend = jnp.cumsum(padded)
    pstart = pend - padded
    dest = pstart[se] + jnp.arange(n_assign, dtype=jnp.int32) - start[se]
    n_slots = -(-(n_assign + N_EXPERTS * (MOE_BLOCK - 1)) // MOE_BLOCK) * MOE_BLOCK
    n_blk = n_slots // MOE_BLOCK
    buf_t = jnp.full((n_slots,), T, jnp.int32).at[dest].set(st)
    buf_w = jnp.zeros((n_slots,), jnp.float32).at[dest].set(sw)
    blk_e = jnp.minimum(jnp.searchsorted(pend, jnp.arange(n_blk) * MOE_BLOCK, side='right'),
                        N_EXPERTS - 1)
    h_pad = jnp.concatenate([h, jnp.zeros((1, d), h.dtype)], axis=0)

    def run_block(args):
        tok, w, e = args
        xb = h_pad[tok]
        y = (jax.nn.silu(xb @ w_gate[e]) * (xb @ w_up[e])) @ w_down[e]
        return y * w[:, None].astype(y.dtype)

    yb = lax.map(run_block, (buf_t.reshape(n_blk, MOE_BLOCK),
                             buf_w.reshape(n_blk, MOE_BLOCK), blk_e))
    out = jax.ops.segment_sum(yb.reshape(n_slots, d), buf_t, num_segments=T + 1)
    return out[:T]


def trunk_layer(l, last, x, xc, c, c_ctx, ang_r, ang_c, ada_w, ada_b, norm1_g, norm2_g,
                w_in, q_norm_g, k_norm_g, lam_q1, lam_k1, lam_q2, lam_k2, subln_g,
                pool_w, pool_scale, gmlp_norm_g, gmlp_ws, gmlp_bs, w_out,
                router_g_w, router_g_b, router_e_w, router_e_b, w_gate, w_up, w_down):
    b, n, d = x.shape
    m = (jax.nn.silu(c) @ ada_w + ada_b)[:, None, :]
    sh1, sc1, g1, sh2, sc2, g2 = jnp.split(m, 6, axis=-1)
    n_cmod = 2 if last else 6
    mc = jax.nn.silu(c_ctx) @ ada_w[:, :n_cmod * d] + ada_b[:n_cmod * d]
    cmods = jnp.split(mc, n_cmod, axis=-1)
    csh1, csc1 = cmods[0], cmods[1]

    lam_init = 0.8 - 0.6 * math.exp(-0.3 * l)
    lam = (jnp.exp(jnp.sum(lam_q1.astype(jnp.float32) * lam_k1.astype(jnp.float32)))
           - jnp.exp(jnp.sum(lam_q2.astype(jnp.float32) * lam_k2.astype(jnp.float32)))
           + lam_init)

    h = rms_norm(x, norm1_g) * (1 + sc1) + sh1
    hc = rms_norm(xc, norm1_g) * (1 + csc1) + csh1
    proj = h @ w_in
    if last:
        kv_c = hc @ w_in[:, K_OFF:POOL_OFF]
        kc_raw, vc_raw = kv_c[..., :ATTN_WIDTH], kv_c[..., ATTN_WIDTH:]
    else:
        proj_c = hc @ w_in
        kc_raw, vc_raw = proj_c[..., K_OFF:V_OFF], proj_c[..., V_OFF:POOL_OFF]
    kc = rms_norm(split_qk(kc_raw), k_norm_g)
    vc = split_v(vc_raw)

    q = rope_2d(rms_norm(split_qk(proj[..., Q_OFF:K_OFF]), q_norm_g), ang_r, ang_c)
    k = rope_2d(rms_norm(split_qk(proj[..., K_OFF:V_OFF]), k_norm_g), ang_r, ang_c)
    v = split_v(proj[..., V_OFF:POOL_OFF])
    k_all = jnp.concatenate([kc, k], axis=1)
    v_all = jnp.concatenate([vc, v], axis=1)
    nb = n // Q_BLOCK
    q_blocks = jnp.moveaxis(q.reshape(b, nb, Q_BLOCK, ATTN_HEADS, 2, ATTN_QK_DIM), 1, 0)
    o = lax.map(lambda qb: diff_attend(qb, k_all, v_all, lam), q_blocks)
    o = jnp.moveaxis(o, 0, 1).reshape(b, n, ATTN_HEADS, ATTN_V_DIM)
    y_attn = diff_head_out(o, subln_g, lam_init)
    y_pool, y_gmlp = local_mixers(proj, pool_w, pool_scale, gmlp_norm_g, gmlp_ws, gmlp_bs)
    x = x + g1 * (jnp.concatenate([y_attn, y_pool, y_gmlp], axis=-1) @ w_out)

    if not last:
        cg1, csh2, csc2, cg2 = cmods[2], cmods[3], cmods[4], cmods[5]
        qc = rms_norm(split_qk(proj_c[..., Q_OFF:K_OFF]), q_norm_g)
        yc_attn = diff_head_out(diff_attend(qc, kc, vc, lam), subln_g, lam_init)
        yc_pool, yc_gmlp = local_mixers(proj_c, pool_w, pool_scale, gmlp_norm_g, gmlp_ws, gmlp_bs)
        xc = xc + cg1 * (jnp.concatenate([yc_attn, yc_pool, yc_gmlp], axis=-1) @ w_out)

    h2 = rms_norm(x, norm2_g) * (1 + sc2) + sh2
    tokens = h2.reshape(b * n, d)
    if not last:
        hc2 = rms_norm(xc, norm2_g) * (1 + csc2) + csh2
        tokens = jnp.concatenate([tokens, hc2.reshape(-1, d)], axis=0)
    f = hier_moe(tokens, router_g_w, router_g_b, router_e_w, router_e_b, w_gate, w_up, w_down)
    x = x + g2 * f[:b * n].reshape(b, n, d)
    if not last:
        xc = xc + cg2 * f[b * n:].reshape(xc.shape)
    return x, xc


def setup_inputs(seed: int = 0) -> dict:
    key = jax.random.key(seed)
    ks = iter(jax.random.split(key, 40))
    L, D = DEPTH, D_MODEL

    def nrm(shape, scale):
        return jax.random.normal(next(ks), shape, jnp.float32) * scale

    def gain(shape):
        return 1.0 + nrm(shape, 0.1)

    return {
        "x": nrm((BATCH, SEQ, D), 1.0),
        "c": nrm((BATCH, D), 1.0),
        "ctx": nrm((BATCH, CTX_LEN, D), 1.0),
        "c_ctx": nrm((D,), 1.0),
        "ada_w": nrm((L, D, 6 * D), 0.5 * D ** -0.5),
        "ada_b": nrm((L, 6 * D), 0.02),
        "norm1_g": gain((L, D)),
        "norm2_g": gain((L, D)),
        "w_in": nrm((L, D, IN_COLS), D ** -0.5),
        "q_norm_g": gain((L, ATTN_QK_DIM)),
        "k_norm_g": gain((L, ATTN_QK_DIM)),
        "lam_q1": nrm((L, ATTN_QK_DIM), 0.1),
        "lam_k1": nrm((L, ATTN_QK_DIM), 0.1),
        "lam_q2": nrm((L, ATTN_QK_DIM), 0.1),
        "lam_k2": nrm((L, ATTN_QK_DIM), 0.1),
        "subln_g": gain((L, ATTN_V_DIM)),
        "pool_w": nrm((L, len(POOL_WINDOWS), POOL_GROUP, POOL_GROUP), POOL_GROUP ** -0.5),
        "pool_scale": gain((L, POOL_WIDTH)),
        "gmlp_norm_g": gain((L, GMLP_WIDTH)),
        "gmlp_ws": nrm((L, GMLP_GROUPS, GMLP_CHUNK, GMLP_CHUNK), GMLP_CHUNK ** -0.5),
        "gmlp_bs": gain((L, GMLP_GROUPS, GMLP_CHUNK)),
        "w_out": nrm((L, MIX_WIDTH, D), MIX_WIDTH ** -0.5),
        "router_g_w": nrm((L, D, N_GROUPS), D ** -0.5),
        "router_g_b": nrm((L, N_GROUPS), 0.01),
        "router_e_w": nrm((L, D, N_EXPERTS), D ** -0.5),
        "router_e_b": nrm((L, N_EXPERTS), 0.01),
        "w_gate": nrm((L, N_EXPERTS, D, EXPERT_FF), D ** -0.5),
        "w_up": nrm((L, N_EXPERTS, D, EXPERT_FF), D ** -0.5),
        "w_down": nrm((L, N_EXPERTS, EXPERT_FF, D), EXPERT_FF ** -0.5),
    }


def reference(x, c, ctx, c_ctx, ada_w, ada_b, norm1_g, norm2_g, w_in, q_norm_g, k_norm_g,
              lam_q1, lam_k1, lam_q2, lam_k2, subln_g, pool_w, pool_scale, gmlp_norm_g,
              gmlp_ws, gmlp_bs, w_out, router_g_w, router_g_b, router_e_w, router_e_b,
              w_gate, w_up, w_down):
    ang_r, ang_c = axial_angles(x.shape[1])
    xc = ctx
    for l in range(DEPTH):
        x, xc = trunk_layer(
            l, l == DEPTH - 1, x, xc, c, c_ctx, ang_r, ang_c,
            ada_w[l], ada_b[l], norm1_g[l], norm2_g[l], w_in[l], q_norm_g[l], k_norm_g[l],
            lam_q1[l], lam_k1[l], lam_q2[l], lam_k2[l], subln_g[l], pool_w[l], pool_scale[l],
            gmlp_norm_g[l], gmlp_ws[l], gmlp_bs[l], w_out[l], router_g_w[l], router_g_b[l],
            router_e_w[l], router_e_b[l], w_gate[l], w_up[l], w_down[l])
    return x
```

```python
import functools
import math

import jax
import jax.numpy as jnp
from jax import lax
from jax.experimental import pallas as pl
from jax.experimental.pallas import tpu as pltpu

F32 = jnp.float32
BF16 = jnp.bfloat16

EPS = 1e-6
GRID_W = 64
ROPE_BASE = 10000.0
N_HEADS = 8
HEAD_DIM = 128
QK_DIM = 64
ROPE_HALF = 16
ATTN_WIDTH = N_HEADS * HEAD_DIM
POOL_WINDOWS = (2, 4, 8, 16)
POOL_GROUP = 128
POOL_WIDTH = 512
POOL_HALO = 8
GMLP_WIDTH = 512
GMLP_GROUPS = 4
GMLP_CHUNK = 128
N_GROUPS = 4
EXPERTS_PER_GROUP = 8
N_EXPERTS = 32
MOE_BLOCK = 256
KEY_BLOCK = 256
LANES = 128
NEG_BIG = -1e30

VMEM_LIMIT_V7X = 56 * 1024 * 1024

ROW_TILE = 256
MM_ROW_TILE = 512
Q_TILE = 256


def _cparams(semantics, vmem=VMEM_LIMIT_V7X):
    return pltpu.CompilerParams(dimension_semantics=semantics, vmem_limit_bytes=vmem)


def _gelu_tanh(x):
    return 0.5 * x * (1.0 + jnp.tanh(math.sqrt(2.0 / math.pi) * (x + 0.044715 * x * x * x)))


def _ada_kernel(ct_ref, w_ref, b_ref, o_ref, *, nseg):
    ct = ct_ref[...]
    at = ct * jax.nn.sigmoid(ct)
    d = w_ref.shape[0]
    chunk = 256
    accs = [jnp.zeros((1, w_ref.shape[1]), F32) for _ in range(nseg)]
    for kc in range(d // chunk):
        w = w_ref[kc * chunk:(kc + 1) * chunk, :]
        for r in range(nseg):
            accs[r] = accs[r] + jnp.sum(at[kc * chunk:(kc + 1) * chunk, r:r + 1] * w,
                                        axis=0, keepdims=True)
    o_ref[...] = jnp.zeros(o_ref.shape, F32)
    for r in range(nseg):
        o_ref[r:r + 1, :] = accs[r] + b_ref[...]


def _ada(c_all, ada_w, ada_b):
    nseg, d = c_all.shape
    n_out = ada_w.shape[1]
    tn = 768
    ct = jnp.zeros((d, 8), F32).at[:, :nseg].set(c_all.T)
    out = pl.pallas_call(
        functools.partial(_ada_kernel, nseg=nseg),
        out_shape=jax.ShapeDtypeStruct((8, n_out), F32),
        grid=(n_out // tn,),
        in_specs=[pl.BlockSpec((d, 8), lambda j: (0, 0)),
                  pl.BlockSpec((d, tn), lambda j: (0, j)),
                  pl.BlockSpec((1, tn), lambda j: (0, j))],
        out_specs=pl.BlockSpec((8, tn), lambda j: (0, j)),
        compiler_params=_cparams(("arbitrary",)),
        name="ada_mod",
    )(ct, ada_w, ada_b.reshape(1, n_out))
    return out[:nseg].reshape(nseg, 6, d)


def _inproj_kernel(x_ref, g_ref, mod_ref, w_ref, o_ref, h_ref):
    @pl.when(pl.program_id(1) == 0)
    def _():
        x = x_ref[...]
        ms = jnp.mean(x * x, axis=-1, keepdims=True)
        y = x * lax.rsqrt(ms + EPS) * g_ref[...]
        h_ref[...] = (y * (1.0 + mod_ref[0, 1:2, :]) + mod_ref[0, 0:1, :]).astype(BF16)

    o_ref[...] = jnp.dot(h_ref[...], w_ref[...], preferred_element_type=F32)


def _inproj(x, norm_g, mod, w_bf16, seg_of_tile):
    t, d = x.shape
    n_out = w_bf16.shape[1]
    tm, tn = MM_ROW_TILE, 1536
    return pl.pallas_call(
        _inproj_kernel,
        out_shape=jax.ShapeDtypeStruct((t, n_out), F32),
        grid=(t // tm, n_out // tn),
        in_specs=[pl.BlockSpec((tm, d), lambda i, j: (i, 0)),
                  pl.BlockSpec((1, d), lambda i, j: (0, 0)),
                  pl.BlockSpec((1, 6, d), lambda i, j: (seg_of_tile(i, tm), 0, 0)),
                  pl.BlockSpec((d, tn), lambda i, j: (0, j))],
        out_specs=pl.BlockSpec((tm, tn), lambda i, j: (i, j)),
        scratch_shapes=[pltpu.VMEM((tm, d), BF16)],
        compiler_params=_cparams(("arbitrary", "arbitrary")),
        name="in_proj",
    )(x, norm_g.reshape(1, d), mod, w_bf16)


def _prep_kernel(q_ref, k_ref, v_ref, cos_ref, sin_ref, qg_ref, kg_ref,
                 qo_ref, ko_ref, vo_ref):
    tm = q_ref.shape[0]
    cos = cos_ref[...]
    sin = sin_ref[...]
    lane = lax.broadcasted_iota(jnp.int32, (tm, LANES), 1)
    first_half = (lane % (2 * ROPE_HALF)) < ROPE_HALF
    r_i = lax.broadcasted_iota(jnp.int32, (LANES, LANES), 0) // QK_DIM
    c_i = lax.broadcasted_iota(jnp.int32, (LANES, LANES), 1) // QK_DIM
    group_ones = jnp.where(r_i == c_i, 1.0, 0.0).astype(BF16)

    def norm_rope(x, g, scale):
        sq = x * x
        hi = sq.astype(BF16)
        lo = (sq - hi.astype(F32)).astype(BF16)
        ss = (jnp.dot(hi, group_ones, preferred_element_type=F32)
              + jnp.dot(lo, group_ones, preferred_element_type=F32))
        y = x * lax.rsqrt(ss * (1.0 / QK_DIM) + EPS) * g
        partner = jnp.where(first_half,
                            pltpu.roll(y, LANES - ROPE_HALF, 1),
                            pltpu.roll(y, ROPE_HALF, 1))
        return (y * cos + partner * sin) * scale

    qg = qg_ref[...]
    kg = kg_ref[...]
    for h in range(N_HEADS):
        sl = slice(h * HEAD_DIM, (h + 1) * HEAD_DIM)
        qo_ref[:, sl] = norm_rope(q_ref[:, sl], qg, QK_DIM ** -0.5).astype(BF16)
        ko_ref[0, :, sl] = norm_rope(k_ref[:, sl], kg, 1.0).astype(BF16)
        vo_ref[0, 0, sl, :] = v_ref[:, sl].T.astype(BF16)


def _prep(proj, cos_t, sin_t, qg, kg, *, batch, n, lc):
    t = proj.shape[0]
    tm = KEY_BLOCK
    nt, nc = n // tm, lc // tm
    n_lat_tiles = batch * nt
    lk = lc + n
    nkb = lk // KEY_BLOCK

    def kv_block(i):
        lat = i < n_lat_tiles
        ic = i - n_lat_tiles
        b = jnp.where(lat, i // nt, ic // nc)
        kb = jnp.where(lat, nc + i % nt, ic % nc)
        return b, kb

    def tab_block(i):
        return jnp.where(i < n_lat_tiles, i % nt, nt)

    return pl.pallas_call(
        _prep_kernel,
        out_shape=(jax.ShapeDtypeStruct((t, ATTN_WIDTH), BF16),
                   jax.ShapeDtypeStruct((batch, lk, ATTN_WIDTH), BF16),
                   jax.ShapeDtypeStruct((batch, nkb, ATTN_WIDTH, KEY_BLOCK), BF16)),
        grid=(t // tm,),
        in_specs=[pl.BlockSpec((tm, ATTN_WIDTH), lambda i: (i, 0)),
                  pl.BlockSpec((tm, ATTN_WIDTH), lambda i: (i, 1)),
                  pl.BlockSpec((tm, ATTN_WIDTH), lambda i: (i, 2)),
                  pl.BlockSpec((tm, LANES), lambda i: (tab_block(i), 0)),
                  pl.BlockSpec((tm, LANES), lambda i: (tab_block(i), 0)),
                  pl.BlockSpec((1, LANES), lambda i: (0, 0)),
                  pl.BlockSpec((1, LANES), lambda i: (0, 0))],
        out_specs=(pl.BlockSpec((tm, ATTN_WIDTH), lambda i: (i, 0)),
                   pl.BlockSpec((1, tm, ATTN_WIDTH), lambda i: kv_block(i) + (0,)),
                   pl.BlockSpec((1, 1, ATTN_WIDTH, KEY_BLOCK),
                                lambda i: kv_block(i) + (0, 0))),
        compiler_params=_cparams(("arbitrary",)),
        name="qkv_prep",
    )(proj, proj, proj, cos_t, sin_t, qg, kg)


def _rope_tables(n, tm):
    pos = jnp.arange(n, dtype=jnp.int32)
    row = (pos // GRID_W).astype(F32)
    col = (pos % GRID_W).astype(F32)
    inv = ROPE_BASE ** (-jnp.arange(ROPE_HALF, dtype=F32) / ROPE_HALF)
    ang_r = row[:, None] * inv[None, :]
    ang_c = col[:, None] * inv[None, :]
    ang = jnp.concatenate([ang_r, ang_r, ang_c, ang_c], axis=-1)
    sign = jnp.tile(jnp.concatenate([-jnp.ones(ROPE_HALF, F32), jnp.ones(ROPE_HALF, F32)]), 2)
    cos = jnp.tile(jnp.cos(ang), (1, 2))
    sin = jnp.tile(jnp.sin(ang) * sign[None, :], (1, 2))
    cos = jnp.concatenate([cos, jnp.ones((tm, LANES), F32)], axis=0)
    sin = jnp.concatenate([sin, jnp.zeros((tm, LANES), F32)], axis=0)
    return cos, sin


def _attn_kernel(lam_ref, q_ref, k_ref, vt_ref, g_ref, o_ref, acc_ref, m_ref, l_ref,
                 *, n_tiles, kb_per_tile, out_scale):
    tq = q_ref.shape[0]
    tk = kb_per_tile * KEY_BLOCK
    q = q_ref[...]
    lane = lax.broadcasted_iota(jnp.int32, (tq, HEAD_DIM), 1)
    zero = jnp.zeros_like(q)
    qbd = jnp.concatenate([jnp.where(lane < QK_DIM, q, zero),
                           jnp.where(lane >= QK_DIM, q, zero)], axis=0)

    m_ref[...] = jnp.full(m_ref.shape, NEG_BIG, F32)
    l_ref[...] = jnp.zeros(l_ref.shape, F32)
    acc_ref[...] = jnp.zeros(acc_ref.shape, F32)

    def body(j, carry):
        kt = k_ref[0, pl.ds(pl.multiple_of(j * tk, tk), tk), :]
        s = lax.dot_general(kt, qbd, (((1,), (1,)), ((), ())),
                            preferred_element_type=F32)
        m_old = m_ref[...]
        m_new = jnp.maximum(m_old, jnp.max(s, axis=0, keepdims=True))
        alpha = jnp.exp(m_old - m_new)
        p = jnp.exp(s - m_new)
        l_ref[...] = alpha * l_ref[...] + jnp.sum(p, axis=0, keepdims=True)
        pb = p.astype(BF16)
        pv = jnp.dot(vt_ref[0, j * kb_per_tile], pb[0:KEY_BLOCK],
                     preferred_element_type=F32)
        for c in range(1, kb_per_tile):
            pv = pv + jnp.dot(vt_ref[0, j * kb_per_tile + c],
                              pb[c * KEY_BLOCK:(c + 1) * KEY_BLOCK],
                              preferred_element_type=F32)
        acc_ref[...] = alpha * acc_ref[...] + pv
        m_ref[...] = m_new
        return carry

    lax.fori_loop(0, n_tiles, body, 0)

    o = acc_ref[...] / l_ref[...]
    d = o[:, :tq] - lam_ref[0] * o[:, tq:]
    ms = jnp.mean(d * d, axis=0, keepdims=True)
    y = d * lax.rsqrt(ms + EPS) * (g_ref[...] * out_scale)
    o_ref[...] = y.T.astype(BF16)


def _attention(lam, q_all, k_all, vt_all, subln_g, *, batch, q_row0, lq, lk, lam_init):
    tq = Q_TILE
    nq = lq // tq
    nkb = lk // KEY_BLOCK
    kb_per_tile = max(c for c in range(1, 6) if nkb % c == 0)
    n_tiles = nkb // kb_per_tile
    qb0 = q_row0 // tq
    kernel = functools.partial(_attn_kernel, n_tiles=n_tiles, kb_per_tile=kb_per_tile,
                               out_scale=1.0 - lam_init)
    return pl.pallas_call(
        kernel,
        out_shape=jax.ShapeDtypeStruct((batch * lq, ATTN_WIDTH), BF16),
        grid=(batch, N_HEADS, nq),
        in_specs=[pl.BlockSpec(memory_space=pltpu.SMEM),
                  pl.BlockSpec((tq, HEAD_DIM), lambda b, h, qi: (qb0 + b * nq + qi, h)),
                  pl.BlockSpec((1, lk, HEAD_DIM), lambda b, h, qi: (b, 0, h)),
                  pl.BlockSpec((1, nkb, HEAD_DIM, KEY_BLOCK), lambda b, h, qi: (b, 0, h, 0)),
                  pl.BlockSpec((HEAD_DIM, 1), lambda b, h, qi: (0, 0))],
        out_specs=pl.BlockSpec((tq, HEAD_DIM), lambda b, h, qi: (b * nq + qi, h)),
        scratch_shapes=[pltpu.VMEM((HEAD_DIM, 2 * tq), F32),
                        pltpu.VMEM((1, 2 * tq), F32),
                        pltpu.VMEM((1, 2 * tq), F32)],
        compiler_params=_cparams(("arbitrary", "arbitrary", "arbitrary")),
        name="diff_attn",
    )(lam.reshape(1), q_all, k_all, vt_all, subln_g.reshape(HEAD_DIM, 1))


def _mixers_kernel(p_ref, pprev_ref, pnext_ref, u_ref, v_ref, pw_ref, ps_ref,
                   ng_ref, ws_ref, bst_ref, yp_ref, yg_ref, *, tiles_per_seq):
    tm = p_ref.shape[0]
    i = pl.program_id(0)
    nt, n_lat, nc = tiles_per_seq
    lat = i < n_lat
    pos = jnp.where(lat, i % nt, (i - n_lat) % nc)
    per = jnp.where(lat, nt, nc)
    first = pos == 0
    last = pos == per - 1

    r = lax.broadcasted_iota(jnp.int32, (tm, tm), 0)
    u = lax.broadcasted_iota(jnp.int32, (tm, tm), 1)
    rh = lax.broadcasted_iota(jnp.int32, (tm, 2 * POOL_HALO), 0)
    uh = lax.broadcasted_iota(jnp.int32, (tm, 2 * POOL_HALO), 1)
    uh_pos = jnp.where(uh < POOL_HALO, uh - POOL_HALO, tm + uh - POOL_HALO)
    uh_ok = jnp.where(uh < POOL_HALO, 1 - first.astype(jnp.int32), 1 - last.astype(jnp.int32)) > 0
    rcol = lax.broadcasted_iota(jnp.int32, (tm, 1), 0)
    seq_lo = jnp.where(first, 0, -POOL_HALO)
    seq_hi = jnp.where(last, tm - 1, tm - 1 + POOL_HALO)

    p = p_ref[...]
    halo = jnp.concatenate([pprev_ref[...], pnext_ref[...]], axis=0)

    def split(x):
        hi = x.astype(BF16)
        return hi, (x - hi.astype(F32)).astype(BF16)

    p_hi, p_lo = split(p)
    h_hi, h_lo = split(halo)
    for gi, win in enumerate(POOL_WINDOWS):
        sl = slice(gi * POOL_GROUP, (gi + 1) * POOL_GROUP)
        lo_off, hi_off = win // 2, win - 1 - win // 2
        band_c = jnp.where((u >= r - lo_off) & (u <= r + hi_off), 1.0, 0.0).astype(BF16)
        band_h = jnp.where((uh_pos >= rh - lo_off) & (uh_pos <= rh + hi_off) & uh_ok,
                           1.0, 0.0).astype(BF16)
        tot = (jnp.dot(band_c, p_hi[:, sl], preferred_element_type=F32)
               + jnp.dot(band_c, p_lo[:, sl], preferred_element_type=F32)
               + jnp.dot(band_h, h_hi[:, sl], preferred_element_type=F32)
               + jnp.dot(band_h, h_lo[:, sl], preferred_element_type=F32))
        lo = jnp.maximum(rcol - lo_off, seq_lo)
        hi = jnp.minimum(rcol + hi_off, seq_hi)
        cnt = (hi - lo + 1).astype(F32)
        dlt = tot / cnt - p[:, sl]
        y = jnp.dot(dlt.astype(BF16), pw_ref[gi], preferred_element_type=F32)
        yp_ref[:, sl] = (y * ps_ref[:, sl]).astype(BF16)

    uu = _gelu_tanh(u_ref[...])
    vv = _gelu_tanh(v_ref[...])
    mu = jnp.mean(vv, axis=-1, keepdims=True)
    vc = vv - mu
    vn = (vc * lax.rsqrt(jnp.mean(vc * vc, axis=-1, keepdims=True) + EPS) * ng_ref[...]).astype(BF16)
    for c in range(tm // GMLP_CHUNK):
        rows = slice(c * GMLP_CHUNK, (c + 1) * GMLP_CHUNK)
        for g in range(GMLP_GROUPS):
            sl = slice(g * GMLP_CHUNK, (g + 1) * GMLP_CHUNK)
            sv = jnp.dot(ws_ref[g], vn[rows, sl], preferred_element_type=F32) + bst_ref[:, g:g + 1]
            yg_ref[rows, sl] = (uu[rows, sl] * sv).astype(BF16)


def _mixers(proj, pool_w_bf16, pool_scale, gmlp_norm_g, gmlp_ws_bf16, gmlp_bs, *, t_rows,
            batch, n, lc):
    tm = ROW_TILE
    nt, nc = n // tm, lc // tm
    n_lat = batch * nt
    hb = tm // POOL_HALO
    n_hblocks = proj.shape[0] // POOL_HALO
    pool_cb, gu_cb, gv_cb = 3 * ATTN_WIDTH // POOL_WIDTH, 3 * ATTN_WIDTH // POOL_WIDTH + 1, \
        3 * ATTN_WIDTH // POOL_WIDTH + 2
    kernel = functools.partial(_mixers_kernel, tiles_per_seq=(nt, n_lat, nc))
    return pl.pallas_call(
        kernel,
        out_shape=(jax.ShapeDtypeStruct((t_rows, POOL_WIDTH), BF16),
                   jax.ShapeDtypeStruct((t_rows, GMLP_WIDTH), BF16)),
        grid=(t_rows // tm,),
        in_specs=[pl.BlockSpec((tm, POOL_WIDTH), lambda i: (i, pool_cb)),
                  pl.BlockSpec((POOL_HALO, POOL_WIDTH),
                               lambda i: (jnp.maximum(i * hb - 1, 0), pool_cb)),
                  pl.BlockSpec((POOL_HALO, POOL_WIDTH),
                               lambda i: (jnp.minimum((i + 1) * hb, n_hblocks - 1), pool_cb)),
                  pl.BlockSpec((tm, GMLP_WIDTH), lambda i: (i, gu_cb)),
                  pl.BlockSpec((tm, GMLP_WIDTH), lambda i: (i, gv_cb)),
                  pl.BlockSpec((len(POOL_WINDOWS), POOL_GROUP, POOL_GROUP), lambda i: (0, 0, 0)),
                  pl.BlockSpec((1, POOL_WIDTH), lambda i: (0, 0)),
                  pl.BlockSpec((1, GMLP_WIDTH), lambda i: (0, 0)),
                  pl.BlockSpec((GMLP_GROUPS, GMLP_CHUNK, GMLP_CHUNK), lambda i: (0, 0, 0)),
                  pl.BlockSpec((GMLP_CHUNK, GMLP_GROUPS), lambda i: (0, 0))],
        out_specs=(pl.BlockSpec((tm, POOL_WIDTH), lambda i: (i, 0)),
                   pl.BlockSpec((tm, GMLP_WIDTH), lambda i: (i, 0))),
        compiler_params=_cparams(("arbitrary",)),
        name="local_mixers",
    )(proj, proj, proj, proj, proj, pool_w_bf16, pool_scale.reshape(1, POOL_WIDTH),
      gmlp_norm_g.reshape(1, GMLP_WIDTH), gmlp_ws_bf16, gmlp_bs.T)


def _outproj_kernel(ya_ref, yp_ref, yg_ref, w_ref, x_ref, mod_ref, o_ref):
    acc = jnp.dot(ya_ref[...], w_ref[0:ATTN_WIDTH, :], preferred_element_type=F32)
    acc = acc + jnp.dot(yp_ref[...], w_ref[ATTN_WIDTH:ATTN_WIDTH + POOL_WIDTH, :],
                        preferred_element_type=F32)
    acc = acc + jnp.dot(yg_ref[...], w_ref[ATTN_WIDTH + POOL_WIDTH:, :],
                        preferred_element_type=F32)
    o_ref[...] = x_ref[...] + mod_ref[0, 2:3, :] * acc


def _outproj(y_attn, y_pool, y_gmlp, w_bf16, x, mod, seg_of_tile, *, t_rows):
    d = x.shape[1]
    tm = MM_ROW_TILE
    return pl.pallas_call(
        _outproj_kernel,
        out_shape=jax.ShapeDtypeStruct((t_rows, d), F32),
        grid=(t_rows // tm,),
        in_specs=[pl.BlockSpec((tm, ATTN_WIDTH), lambda i: (i, 0)),
                  pl.BlockSpec((tm, POOL_WIDTH), lambda i: (i, 0)),
                  pl.BlockSpec((tm, GMLP_WIDTH), lambda i: (i, 0)),
                  pl.BlockSpec(w_bf16.shape, lambda i: (0, 0)),
                  pl.BlockSpec((tm, d), lambda i: (i, 0)),
                  pl.BlockSpec((1, 6, d), lambda i: (seg_of_tile(i, tm), 0, 0))],
        out_specs=pl.BlockSpec((tm, d), lambda i: (i, 0)),
        compiler_params=_cparams(("arbitrary",)),
        name="out_proj",
    )(y_attn, y_pool, y_gmlp, w_bf16, x, mod)


def _router_kernel(x_ref, g_ref, mod_ref, wgh_ref, wgl_ref, weh_ref, wel_ref, bg_ref, be_ref,
                   h_ref, route_ref, cnt_ref, carry_ref):
    tm = x_ref.shape[0]

    @pl.when(pl.program_id(0) == 0)
    def _():
        carry_ref[...] = jnp.zeros(carry_ref.shape, F32)

    x = x_ref[...]
    ms = jnp.mean(x * x, axis=-1, keepdims=True)
    y = x * lax.rsqrt(ms + EPS) * g_ref[...]
    h = y * (1.0 + mod_ref[0, 4:5, :]) + mod_ref[0, 3:4, :]
    h_ref[...] = h

    hh = h.astype(BF16)
    hl = (h - hh.astype(F32)).astype(BF16)

    def logits(w_hi, w_lo, b):
        return (jnp.dot(hh, w_hi[...], preferred_element_type=F32)
                + jnp.dot(hl, w_hi[...], preferred_element_type=F32)
                + jnp.dot(hh, w_lo[...], preferred_element_type=F32) + b[...])

    gl = logits(wgh_ref, wgl_ref, bg_ref)
    el = logits(weh_ref, wel_ref, be_ref)
    lane = lax.broadcasted_iota(jnp.int32, (tm, LANES), 1).astype(F32)
    far = float(LANES)

    def first_argmax(v):
        vmax = jnp.max(v, axis=-1, keepdims=True)
        idx = jnp.min(jnp.where(v == vmax, lane, far), axis=-1, keepdims=True)
        return vmax, idx

    glm = jnp.where(lane < N_GROUPS, gl, NEG_BIG)
    gmax, g_idx = first_argmax(glm)
    g_val = 1.0 / jnp.sum(jnp.exp(glm - gmax), axis=-1, keepdims=True)
    in_grp = (jnp.floor(lane * (1.0 / EXPERTS_PER_GROUP)) == g_idx) & (lane < N_EXPERTS)
    elm = jnp.where(in_grp, el, NEG_BIG)
    m1, e1 = first_argmax(elm)
    elm2 = jnp.where(lane == e1, NEG_BIG, elm)
    m2, e2 = first_argmax(elm2)
    rr = jnp.exp(m2 - m1)
    w1 = g_val / (1.0 + rr)
    w2 = g_val * rr / (1.0 + rr)

    sel1 = lane == e1
    sel2 = lane == e2
    esum = jnp.where(sel1 | sel2, 1.0, 0.0)
    rI = lax.broadcasted_iota(jnp.int32, (tm, tm), 0)
    cI = lax.broadcasted_iota(jnp.int32, (tm, tm), 1)
    ltri = jnp.where(rI > cI, 1.0, 0.0).astype(BF16)
    before = jnp.dot(ltri, esum.astype(BF16), preferred_element_type=F32) + carry_ref[0:1, :]
    rank1 = jnp.sum(jnp.where(sel1, before, 0.0), axis=-1, keepdims=True)
    rank2 = jnp.sum(jnp.where(sel2, before, 0.0), axis=-1, keepdims=True)
    carry_ref[0:1, :] = carry_ref[0:1, :] + jnp.sum(esum, axis=0, keepdims=True)
    cnt_ref[...] = carry_ref[...]

    out = jnp.where(lane == 0, e1, 0.0)
    out = jnp.where(lane == 1, e2, out)
    out = jnp.where(lane == 2, rank1, out)
    out = jnp.where(lane == 3, rank2, out)
    out = jnp.where(lane == 4, w1, out)
    out = jnp.where(lane == 5, w2, out)
    route_ref[...] = out[:, 0:8]


def _router(x, norm_g, mod, rg_w, rg_b, re_w, re_b, seg_of_tile, *, t_rows):
    d = x.shape[1]
    tm = ROW_TILE

    def pad_split(w):
        wp = jnp.zeros((d, LANES), F32).at[:, :w.shape[1]].set(w)
        hi = wp.astype(BF16)
        return hi, (wp - hi.astype(F32)).astype(BF16)

    def pad_bias(b):
        return jnp.zeros((1, LANES), F32).at[0, :b.shape[0]].set(b)

    wgh, wgl = pad_split(rg_w)
    weh, wel = pad_split(re_w)
    wspec = pl.BlockSpec((d, LANES), lambda i: (0, 0))
    bspec = pl.BlockSpec((1, LANES), lambda i: (0, 0))
    return pl.pallas_call(
        _router_kernel,
        out_shape=(jax.ShapeDtypeStruct((t_rows, d), F32),
                   jax.ShapeDtypeStruct((t_rows, 8), F32),
                   jax.ShapeDtypeStruct((8, LANES), F32)),
        grid=(t_rows // tm,),
        in_specs=[pl.BlockSpec((tm, d), lambda i: (i, 0)),
                  pl.BlockSpec((1, d), lambda i: (0, 0)),
                  pl.BlockSpec((1, 6, d), lambda i: (seg_of_tile(i, tm), 0, 0)),
                  wspec, wspec, wspec, wspec, bspec, bspec],
        out_specs=(pl.BlockSpec((tm, d), lambda i: (i, 0)),
                   pl.BlockSpec((tm, 8), lambda i: (i, 0)),
                   pl.BlockSpec((8, LANES), lambda i: (0, 0))),
        scratch_shapes=[pltpu.VMEM((8, LANES), F32)],
        compiler_params=_cparams(("arbitrary",)),
        name="moe_router",
    )(x, norm_g.reshape(1, d), mod, wgh, wgl, weh, wel, pad_bias(rg_b), pad_bias(re_b))


def _dispatch_kernel(dest_ref, h_ref, xs_in_ref, xs_ref, sem):
    del xs_in_ref
    tm = h_ref.shape[0]
    base = pl.program_id(0) * (2 * tm)

    def row_copy(r, slot):
        return pltpu.make_async_copy(h_ref.at[pl.ds(r, 1)], xs_ref.at[pl.ds(slot, 1)], sem)

    def issue(r, carry):
        row_copy(r, dest_ref[base + 2 * r]).start()
        row_copy(r, dest_ref[base + 2 * r + 1]).start()
        return carry

    def drain(r, carry):
        row_copy(0, 0).wait()
        row_copy(0, 0).wait()
        return carry

    lax.fori_loop(0, tm, issue, 0)
    lax.fori_loop(0, tm, drain, 0)


def _dispatch(dest_flat, h2, n_slots):
    t, d = h2.shape
    tm = ROW_TILE
    return pl.pallas_call(
        _dispatch_kernel,
        out_shape=jax.ShapeDtypeStruct((n_slots, d), F32),
        grid_spec=pltpu.PrefetchScalarGridSpec(
            num_scalar_prefetch=1,
            grid=(t // tm,),
            in_specs=[pl.BlockSpec((tm, d), lambda i, dest: (i, 0)),
                      pl.BlockSpec(memory_space=pl.ANY)],
            out_specs=pl.BlockSpec(memory_space=pl.ANY),
            scratch_shapes=[pltpu.SemaphoreType.DMA]),
        input_output_aliases={2: 0},
        compiler_params=_cparams(("arbitrary",)),
        name="moe_dispatch",
    )(dest_flat, h2, jnp.zeros((n_slots, d), F32))


def _expert_kernel(blk_e_ref, n_used_ref, xs_ref, wg_ref, wu_ref, wd_ref, ys_ref):
    del blk_e_ref
    i = pl.program_id(0)

    @pl.when(i < n_used_ref[0])
    def _():
        xb = xs_ref[...].astype(BF16)
        a = jnp.dot(xb, wg_ref[0], preferred_element_type=F32)
        b = jnp.dot(xb, wu_ref[0], preferred_element_type=F32)
        hmid = (a * jax.nn.sigmoid(a) * b).astype(BF16)
        ys_ref[...] = jnp.dot(hmid, wd_ref[0], preferred_element_type=F32)

    @pl.when(i >= n_used_ref[0])
    def _():
        ys_ref[...] = jnp.zeros(ys_ref.shape, F32)


def _experts(blk_e, n_used, xs, wg_bf16, wu_bf16, wd_bf16):
    n_slots, d = xs.shape
    ff = wg_bf16.shape[2]
    return pl.pallas_call(
        _expert_kernel,
        out_shape=jax.ShapeDtypeStruct((n_slots, d), F32),
        grid_spec=pltpu.PrefetchScalarGridSpec(
            num_scalar_prefetch=2,
            grid=(n_slots // MOE_BLOCK,),
            in_specs=[pl.BlockSpec((MOE_BLOCK, d), lambda i, be, nu: (i, 0)),
                      pl.BlockSpec((1, d, ff), lambda i, be, nu: (be[i], 0, 0)),
                      pl.BlockSpec((1, d, ff), lambda i, be, nu: (be[i], 0, 0)),
                      pl.BlockSpec((1, ff, d), lambda i, be, nu: (be[i], 0, 0))],
            out_specs=pl.BlockSpec((MOE_BLOCK, d), lambda i, be, nu: (i, 0))),
        compiler_params=_cparams(("arbitrary",)),
        name="moe_experts",
    )(blk_e, n_used, xs, wg_bf16, wu_bf16, wd_bf16)


def _combine_kernel(dest_ref, x_ref, mod_ref, route_ref, ys_ref, o_ref, y0_ref, y1_ref, sem):
    tm = x_ref.shape[0]
    base = pl.program_id(0) * (2 * tm)

    def row_copy(slot, buf, r):
        return pltpu.make_async_copy(ys_ref.at[pl.ds(slot, 1)], buf.at[pl.ds(r, 1)], sem)

    def issue(r, carry):
        row_copy(dest_ref[base + 2 * r], y0_ref, r).start()
        row_copy(dest_ref[base + 2 * r + 1], y1_ref, r).start()
        return carry

    def drain(r, carry):
        row_copy(0, y0_ref, 0).wait()
        row_copy(0, y1_ref, 0).wait()
        return carry

    lax.fori_loop(0, tm, issue, 0)
    lax.fori_loop(0, tm, drain, 0)
    f = route_ref[:, 4:5] * y0_ref[...] + route_ref[:, 5:6] * y1_ref[...]
    o_ref[...] = x_ref[...] + mod_ref[0, 5:6, :] * f


def _combine(dest_flat, x, mod, route, ys, seg_of_tile):
    t, d = x.shape
    tm = ROW_TILE
    return pl.pallas_call(
        _combine_kernel,
        out_shape=jax.ShapeDtypeStruct((t, d), F32),
        grid_spec=pltpu.PrefetchScalarGridSpec(
            num_scalar_prefetch=1,
            grid=(t // tm,),
            in_specs=[pl.BlockSpec((tm, d), lambda i, dest: (i, 0)),
                      pl.BlockSpec((1, 6, d), lambda i, dest: (seg_of_tile(i, tm), 0, 0)),
                      pl.BlockSpec((tm, 8), lambda i, dest: (i, 0)),
                      pl.BlockSpec(memory_space=pl.ANY)],
            out_specs=pl.BlockSpec((tm, d), lambda i, dest: (i, 0)),
            scratch_shapes=[pltpu.VMEM((tm, d), F32), pltpu.VMEM((tm, d), F32),
                            pltpu.SemaphoreType.DMA]),
        compiler_params=_cparams(("arbitrary",)),
        name="moe_combine",
    )(dest_flat, x, mod, route, ys)


def _moe(x, norm_g, mod, rg_w, rg_b, re_w, re_b, wg_bf16, wu_bf16, wd_bf16, seg_of_tile):
    t = x.shape[0]
    h2, route, cnt = _router(x, norm_g, mod, rg_w, rg_b, re_w, re_b, seg_of_tile, t_rows=t)
    counts = cnt[0, :N_EXPERTS].astype(jnp.int32)
    padded = (counts + MOE_BLOCK - 1) // MOE_BLOCK * MOE_BLOCK
    pend = jnp.cumsum(padded)
    pstart = pend - padded
    n_slots = -(-(2 * t + N_EXPERTS * (MOE_BLOCK - 1)) // MOE_BLOCK) * MOE_BLOCK
    n_blk = n_slots // MOE_BLOCK
    eid = route[:, 0:2].astype(jnp.int32)
    rank = route[:, 2:4].astype(jnp.int32)
    dest = (pstart[eid] + rank).reshape(-1)
    blk_e = jnp.minimum(jnp.searchsorted(pend, jnp.arange(n_blk, dtype=jnp.int32) * MOE_BLOCK,
                                         side='right'), N_EXPERTS - 1).astype(jnp.int32)
    n_used = (pend[-1] // MOE_BLOCK).astype(jnp.int32).reshape(1)
    xs = _dispatch(dest, h2, n_slots)
    ys = _experts(blk_e, n_used, xs, wg_bf16, wu_bf16, wd_bf16)
    return _combine(dest, x, mod, route, ys, seg_of_tile)


def kernel(x, c, ctx, c_ctx, ada_w, ada_b, norm1_g, norm2_g, w_in, q_norm_g, k_norm_g, lam_q1, lam_k1, lam_q2, lam_k2, subln_g, pool_w, pool_scale, gmlp_norm_g, gmlp_ws, gmlp_bs, w_out, router_g_w, router_g_b, router_e_w, router_e_b, w_gate, w_up, w_down):
    batch, n, d = x.shape
    lc = ctx.shape[1]
    depth = ada_w.shape[0]
    t_lat = batch * n
    t_all = t_lat + batch * lc
    assert n % MM_ROW_TILE == 0 and (batch * lc) % MM_ROW_TILE == 0 and lc % KEY_BLOCK == 0
    assert n % Q_TILE == 0 and lc % Q_TILE == 0 and n % GRID_W == 0

    def seg_of_tile(i, tm):
        return jnp.minimum(i // (n // tm), batch)

    xs = jnp.concatenate([x.reshape(t_lat, d), ctx.reshape(batch * lc, d)], axis=0)
    c_all = jnp.concatenate([c, c_ctx[None, :]], axis=0)
    cos_t, sin_t = _rope_tables(n, KEY_BLOCK)

    for l in range(depth):
        last = l == depth - 1
        lam_init = 0.8 - 0.6 * math.exp(-0.3 * l)
        lam = (jnp.exp(jnp.sum(lam_q1[l] * lam_k1[l])) - jnp.exp(jnp.sum(lam_q2[l] * lam_k2[l]))
               + lam_init).astype(F32)
        mod = _ada(c_all, ada_w[l], ada_b[l])
        proj = _inproj(xs, norm1_g[l], mod, w_in[l].astype(BF16), seg_of_tile)
        q_all, k_all, vt_all = _prep(proj, cos_t, sin_t,
                                     jnp.tile(q_norm_g[l], 2).reshape(1, LANES),
                                     jnp.tile(k_norm_g[l], 2).reshape(1, LANES),
                                     batch=batch, n=n, lc=lc)
        t_rows = t_lat if last else t_all
        y_attn = _attention(lam, q_all, k_all, vt_all, subln_g[l], batch=batch,
                            q_row0=0, lq=n, lk=lc + n, lam_init=lam_init)
        if not last:
            y_ctx = _attention(lam, q_all, k_all, vt_all, subln_g[l], batch=batch,
                               q_row0=t_lat, lq=lc, lk=lc, lam_init=lam_init)
            y_attn = jnp.concatenate([y_attn, y_ctx], axis=0)
        y_pool, y_gmlp = _mixers(proj, pool_w[l].astype(BF16), pool_scale[l], gmlp_norm_g[l],
                                 gmlp_ws[l].astype(BF16), gmlp_bs[l], t_rows=t_rows,
                                 batch=batch, n=n, lc=lc)
        x1 = _outproj(y_attn, y_pool, y_gmlp, w_out[l].astype(BF16), xs, mod, seg_of_tile,
                      t_rows=t_rows)
        xs = _moe(x1, norm2_g[l], mod, router_g_w[l], router_g_b[l], router_e_w[l],
                  router_e_b[l], w_gate[l].astype(BF16), w_up[l].astype(BF16),
                  w_down[l].astype(BF16), seg_of_tile)
    return xs[:t_lat].reshape(batch, n, d)
```

```python
import functools
import math

import jax
import jax.numpy as jnp
from jax import lax
from jax.experimental import pallas as pl
from jax.experimental.pallas import tpu as pltpu

F32 = jnp.float32
BF16 = jnp.bfloat16

EPS = 1e-6
GRID_W = 64
ROPE_BASE = 10000.0
N_HEADS = 8
HEAD_DIM = 128
QK_DIM = 64
ROPE_HALF = 16
ATTN_WIDTH = N_HEADS * HEAD_DIM
POOL_WINDOWS = (2, 4, 8, 16)
POOL_GROUP = 128
POOL_WIDTH = 512
POOL_HALO = 8
GMLP_WIDTH = 512
GMLP_GROUPS = 4
GMLP_CHUNK = 128
N_GROUPS = 4
EXPERTS_PER_GROUP = 8
N_EXPERTS = 32
MOE_BLOCK = 256
KEY_BLOCK = 256
PIPE_BUFS = 2
MAX_KB_PER_TILE = 5
VT_ROWS = HEAD_DIM + 16
LANES = 128
NEG_BIG = -1e30
Q_SCALE_LOG2 = QK_DIM ** -0.5 * math.log2(math.e)

VMEM_LIMIT_V7X = 56 * 1024 * 1024

ROW_TILE = 256
MM_ROW_TILE = 512
Q_TILE = 256


def _cparams(semantics, vmem=VMEM_LIMIT_V7X, flags=None):
    return pltpu.CompilerParams(dimension_semantics=semantics, vmem_limit_bytes=vmem, flags=flags)


def _gelu_tanh(x):
    return 0.5 * x * (1.0 + jnp.tanh(math.sqrt(2.0 / math.pi) * (x + 0.044715 * x * x * x)))


def _ada_kernel(ct_ref, w_ref, b_ref, o_ref, *, nseg):
    ct = ct_ref[...]
    at = ct * jax.nn.sigmoid(ct)
    d = w_ref.shape[0]
    chunk = 256
    accs = [jnp.zeros((1, w_ref.shape[1]), F32) for _ in range(nseg)]
    for kc in range(d // chunk):
        w = w_ref[kc * chunk:(kc + 1) * chunk, :]
        for r in range(nseg):
            accs[r] = accs[r] + jnp.sum(at[kc * chunk:(kc + 1) * chunk, r:r + 1] * w,
                                        axis=0, keepdims=True)
    o_ref[...] = jnp.zeros(o_ref.shape, F32)
    for r in range(nseg):
        o_ref[r:r + 1, :] = accs[r] + b_ref[...]


def _ada(c_all, ada_w, ada_b):
    nseg, d = c_all.shape
    n_out = ada_w.shape[1]
    tn = 768
    ct = jnp.zeros((d, 8), F32).at[:, :nseg].set(c_all.T)
    out = pl.pallas_call(
        functools.partial(_ada_kernel, nseg=nseg),
        out_shape=jax.ShapeDtypeStruct((8, n_out), F32),
        grid=(n_out // tn,),
        in_specs=[pl.BlockSpec((d, 8), lambda j: (0, 0)),
                  pl.BlockSpec((d, tn), lambda j: (0, j)),
                  pl.BlockSpec((1, tn), lambda j: (0, j))],
        out_specs=pl.BlockSpec((8, tn), lambda j: (0, j)),
        compiler_params=_cparams(("arbitrary",)),
        name="ada_mod",
    )(ct, ada_w, ada_b.reshape(1, n_out))
    return out[:nseg].reshape(nseg, 6, d)


def _inproj_kernel(x_ref, g_ref, mod_ref, w_ref, o_ref, h_ref):
    @pl.when(pl.program_id(1) == 0)
    def _():
        x = x_ref[...]
        ms = jnp.mean(x * x, axis=-1, keepdims=True)
        y = x * lax.rsqrt(ms + EPS) * g_ref[...]
        h_ref[...] = (y * (1.0 + mod_ref[0, 1:2, :]) + mod_ref[0, 0:1, :]).astype(BF16)

    o_ref[...] = jnp.dot(h_ref[...], w_ref[...], preferred_element_type=F32)


def _inproj(x, norm_g, mod, w_bf16, seg_of_tile):
    t, d = x.shape
    n_out = w_bf16.shape[1]
    tm, tn = MM_ROW_TILE, 1536
    return pl.pallas_call(
        _inproj_kernel,
        out_shape=jax.ShapeDtypeStruct((t, n_out), F32),
        grid=(t // tm, n_out // tn),
        in_specs=[pl.BlockSpec((tm, d), lambda i, j: (i, 0)),
                  pl.BlockSpec((1, d), lambda i, j: (0, 0)),
                  pl.BlockSpec((1, 6, d), lambda i, j: (seg_of_tile(i, tm), 0, 0)),
                  pl.BlockSpec((d, tn), lambda i, j: (0, j))],
        out_specs=pl.BlockSpec((tm, tn), lambda i, j: (i, j)),
        scratch_shapes=[pltpu.VMEM((tm, d), BF16)],
        compiler_params=_cparams(("arbitrary", "arbitrary")),
        name="in_proj",
    )(x, norm_g.reshape(1, d), mod, w_bf16)


def _prep_kernel(q_ref, k_ref, v_ref, cos_ref, sin_ref, qg_ref, kg_ref,
                 qo_ref, ko_ref, vo_ref):
    tm = q_ref.shape[0]
    cos = cos_ref[...]
    sin = sin_ref[...]
    lane = lax.broadcasted_iota(jnp.int32, (tm, LANES), 1)
    first_half = (lane % (2 * ROPE_HALF)) < ROPE_HALF
    r_i = lax.broadcasted_iota(jnp.int32, (LANES, LANES), 0) // QK_DIM
    c_i = lax.broadcasted_iota(jnp.int32, (LANES, LANES), 1) // QK_DIM
    group_ones = jnp.where(r_i == c_i, 1.0, 0.0).astype(BF16)

    def norm_rope(x, g, scale):
        sq = x * x
        hi = sq.astype(BF16)
        lo = (sq - hi.astype(F32)).astype(BF16)
        ss = (jnp.dot(hi, group_ones, preferred_element_type=F32)
              + jnp.dot(lo, group_ones, preferred_element_type=F32))
        y = x * lax.rsqrt(ss * (1.0 / QK_DIM) + EPS) * g
        partner = jnp.where(first_half,
                            pltpu.roll(y, LANES - ROPE_HALF, 1),
                            pltpu.roll(y, ROPE_HALF, 1))
        return (y * cos + partner * sin) * scale

    qg = qg_ref[...]
    kg = kg_ref[...]
    pad_row = lax.broadcasted_iota(jnp.int32, (VT_ROWS - HEAD_DIM, tm), 0)
    ones_rows = jnp.where(pad_row == 0, 1.0, 0.0).astype(BF16)
    for h in range(N_HEADS):
        sl = slice(h * HEAD_DIM, (h + 1) * HEAD_DIM)
        qo_ref[:, sl] = norm_rope(q_ref[:, sl], qg, Q_SCALE_LOG2).astype(BF16)
        ko_ref[0, :, sl] = norm_rope(k_ref[:, sl], kg, 1.0).astype(BF16)
        vo_ref[0, 0, h * VT_ROWS:h * VT_ROWS + HEAD_DIM, :] = v_ref[:, sl].T.astype(BF16)
        vo_ref[0, 0, h * VT_ROWS + HEAD_DIM:(h + 1) * VT_ROWS, :] = ones_rows


def _prep(proj, cos_t, sin_t, qg, kg, *, batch, n, lc):
    t = proj.shape[0]
    tm = KEY_BLOCK
    nt, nc = n // tm, lc // tm
    n_lat_tiles = batch * nt
    lk = lc + n
    nkb = lk // KEY_BLOCK

    def kv_block(i):
        lat = i < n_lat_tiles
        ic = i - n_lat_tiles
        b = jnp.where(lat, i // nt, ic // nc)
        kb = jnp.where(lat, nc + i % nt, ic % nc)
        return b, kb

    def tab_block(i):
        return jnp.where(i < n_lat_tiles, i % nt, nt)

    return pl.pallas_call(
        _prep_kernel,
        out_shape=(jax.ShapeDtypeStruct((t, ATTN_WIDTH), BF16),
                   jax.ShapeDtypeStruct((batch, lk, ATTN_WIDTH), BF16),
                   jax.ShapeDtypeStruct((batch, nkb, N_HEADS * VT_ROWS, KEY_BLOCK), BF16)),
        grid=(t // tm,),
        in_specs=[pl.BlockSpec((tm, ATTN_WIDTH), lambda i: (i, 0)),
                  pl.BlockSpec((tm, ATTN_WIDTH), lambda i: (i, 1)),
                  pl.BlockSpec((tm, ATTN_WIDTH), lambda i: (i, 2)),
                  pl.BlockSpec((tm, LANES), lambda i: (tab_block(i), 0)),
                  pl.BlockSpec((tm, LANES), lambda i: (tab_block(i), 0)),
                  pl.BlockSpec((1, LANES), lambda i: (0, 0)),
                  pl.BlockSpec((1, LANES), lambda i: (0, 0))],
        out_specs=(pl.BlockSpec((tm, ATTN_WIDTH), lambda i: (i, 0)),
                   pl.BlockSpec((1, tm, ATTN_WIDTH), lambda i: kv_block(i) + (0,)),
                   pl.BlockSpec((1, 1, N_HEADS * VT_ROWS, KEY_BLOCK),
                                lambda i: kv_block(i) + (0, 0))),
        compiler_params=_cparams(("arbitrary",)),
        name="qkv_prep",
    )(proj, proj, proj, cos_t, sin_t, qg, kg)


def _rope_tables(n, tm):
    pos = jnp.arange(n, dtype=jnp.int32)
    row = (pos // GRID_W).astype(F32)
    col = (pos % GRID_W).astype(F32)
    inv = ROPE_BASE ** (-jnp.arange(ROPE_HALF, dtype=F32) / ROPE_HALF)
    ang_r = row[:, None] * inv[None, :]
    ang_c = col[:, None] * inv[None, :]
    ang = jnp.concatenate([ang_r, ang_r, ang_c, ang_c], axis=-1)
    sign = jnp.tile(jnp.concatenate([-jnp.ones(ROPE_HALF, F32), jnp.ones(ROPE_HALF, F32)]), 2)
    cos = jnp.tile(jnp.cos(ang), (1, 2))
    sin = jnp.tile(jnp.sin(ang) * sign[None, :], (1, 2))
    cos = jnp.concatenate([cos, jnp.ones((tm, LANES), F32)], axis=0)
    sin = jnp.concatenate([sin, jnp.zeros((tm, LANES), F32)], axis=0)
    return cos, sin


def _attn_kernel(lam_ref, q_ref, k_ref, vt_ref, g_ref, o_ref, s_ref, p_ref, a_ref, acc_ref, m_ref,
                 *, n_tiles, kb_per_tile, out_scale):
    tq = q_ref.shape[0]
    w = 2 * tq
    q = q_ref[...]
    lane = lax.broadcasted_iota(jnp.int32, (tq, HEAD_DIM), 1)
    zero = jnp.zeros_like(q)
    qbd = jnp.concatenate([jnp.where(lane < QK_DIM, q, zero),
                           jnp.where(lane >= QK_DIM, q, zero)], axis=0)

    m_ref[...] = jnp.full(m_ref.shape, NEG_BIG, F32)
    acc_ref[...] = jnp.zeros(acc_ref.shape, F32)
    sub = KEY_BLOCK // 8

    def score_block(t, slot, c, cmax):
        row0 = pl.multiple_of((t * kb_per_tile + c) * KEY_BLOCK, KEY_BLOCK)
        kt = k_ref[0, pl.ds(row0, KEY_BLOCK), :]
        s = lax.dot_general(kt, qbd, (((1,), (1,)), ((), ())),
                            preferred_element_type=F32)
        s_ref[slot, c * KEY_BLOCK:(c + 1) * KEY_BLOCK, :] = s
        mx = jnp.max(s.reshape(sub, 8, w), axis=0)
        return mx if cmax is None else jnp.maximum(cmax, mx)

    def new_max(slot, cmax):
        m_old = m_ref[...]
        m_new = jnp.maximum(m_old, jnp.max(cmax, axis=0, keepdims=True))
        a_ref[slot] = jnp.exp2(m_old - m_new)
        m_ref[...] = m_new
        return m_new

    def exp_block(slot, c, m_new):
        rows = slice(c * KEY_BLOCK, (c + 1) * KEY_BLOCK)
        p_ref[slot, rows, :] = jnp.exp2((s_ref[slot, rows, :] - m_new).astype(BF16))

    def accumulate(t, slot):
        vt = jnp.concatenate([vt_ref[0, t * kb_per_tile + c] for c in range(kb_per_tile)], axis=1)
        pv = jnp.dot(vt, p_ref[slot], preferred_element_type=F32)
        acc_ref[...] = a_ref[slot] * acc_ref[...] + pv

    def step(t, slot, cmax_next, *, with_scores, with_exp, with_pv=True):
        cm = None
        slot1, slot2 = (slot + 1) % PIPE_BUFS, (slot + 2) % PIPE_BUFS
        m_new = new_max(slot1, cmax_next) if with_exp else None
        for c in range(kb_per_tile):
            if with_scores:
                cm = score_block(t + 2, slot2, c, cm)
            if with_exp:
                exp_block(slot1, c, m_new)
        if with_pv:
            accumulate(t, slot)
        return cm

    cm = None
    for c in range(kb_per_tile):
        cm = score_block(0, 0, c, cm)
    if n_tiles > 1:
        cm = step(-1, PIPE_BUFS - 1, cm, with_scores=True, with_exp=True, with_pv=False)
    else:
        m0 = new_max(0, cm)
        for c in range(kb_per_tile):
            exp_block(0, c, m0)

    n_rounds = max(n_tiles - 2, 0) // PIPE_BUFS

    def full_round(u, cmx):
        for r in range(PIPE_BUFS):
            cmx = step(PIPE_BUFS * u + r, r, cmx, with_scores=True, with_exp=True)
        return cmx

    if n_rounds > 0:
        cm = lax.fori_loop(0, n_rounds, full_round, cm)
    for t in range(PIPE_BUFS * n_rounds, n_tiles):
        cm = step(t, t % PIPE_BUFS, cm, with_scores=t + 2 < n_tiles, with_exp=t + 1 < n_tiles)

    o = acc_ref[0:HEAD_DIM, :] / acc_ref[HEAD_DIM:HEAD_DIM + 1, :]
    d = o[:, :tq] - lam_ref[0] * o[:, tq:]
    ms = jnp.mean(d * d, axis=0, keepdims=True)
    y = d * lax.rsqrt(ms + EPS) * (g_ref[...] * out_scale)
    o_ref[...] = y.T.astype(BF16)


def _attention(lam, q_all, k_all, vt_all, subln_g, *, batch, q_row0, lq, lk, lam_init):
    tq = Q_TILE
    nq = lq // tq
    nkb = lk // KEY_BLOCK
    kb_per_tile = max(c for c in range(1, MAX_KB_PER_TILE + 1) if nkb % c == 0)
    n_tiles = nkb // kb_per_tile
    qb0 = q_row0 // tq
    kernel = functools.partial(_attn_kernel, n_tiles=n_tiles, kb_per_tile=kb_per_tile,
                               out_scale=1.0 - lam_init)
    return pl.pallas_call(
        kernel,
        out_shape=jax.ShapeDtypeStruct((batch * lq, ATTN_WIDTH), BF16),
        grid=(batch, N_HEADS, nq),
        in_specs=[pl.BlockSpec(memory_space=pltpu.SMEM),
                  pl.BlockSpec((tq, HEAD_DIM), lambda b, h, qi: (qb0 + b * nq + qi, h)),
                  pl.BlockSpec((1, lk, HEAD_DIM), lambda b, h, qi: (b, 0, h)),
                  pl.BlockSpec((1, nkb, VT_ROWS, KEY_BLOCK), lambda b, h, qi: (b, 0, h, 0)),
                  pl.BlockSpec((HEAD_DIM, 1), lambda b, h, qi: (0, 0))],
        out_specs=pl.BlockSpec((tq, HEAD_DIM), lambda b, h, qi: (b * nq + qi, h)),
        scratch_shapes=[pltpu.VMEM((PIPE_BUFS, kb_per_tile * KEY_BLOCK, 2 * tq), F32),
                        pltpu.VMEM((PIPE_BUFS, kb_per_tile * KEY_BLOCK, 2 * tq), BF16),
                        pltpu.VMEM((PIPE_BUFS, 1, 2 * tq), F32),
                        pltpu.VMEM((VT_ROWS, 2 * tq), F32),
                        pltpu.VMEM((1, 2 * tq), F32)],
        compiler_params=_cparams(("arbitrary", "arbitrary", "arbitrary")),
        name="diff_attn",
    )(lam.reshape(1), q_all, k_all, vt_all, subln_g.reshape(HEAD_DIM, 1))


def _mixers_kernel(p_ref, pprev_ref, pnext_ref, u_ref, v_ref, pw_ref, ps_ref,
                   ng_ref, ws_ref, bst_ref, yp_ref, yg_ref, *, tiles_per_seq):
    tm = p_ref.shape[0]
    i = pl.program_id(0)
    nt, n_lat, nc = tiles_per_seq
    lat = i < n_lat
    pos = jnp.where(lat, i % nt, (i - n_lat) % nc)
    per = jnp.where(lat, nt, nc)
    first = pos == 0
    last = pos == per - 1

    r = lax.broadcasted_iota(jnp.int32, (tm, tm), 0)
    u = lax.broadcasted_iota(jnp.int32, (tm, tm), 1)
    rh = lax.broadcasted_iota(jnp.int32, (tm, 2 * POOL_HALO), 0)
    uh = lax.broadcasted_iota(jnp.int32, (tm, 2 * POOL_HALO), 1)
    uh_pos = jnp.where(uh < POOL_HALO, uh - POOL_HALO, tm + uh - POOL_HALO)
    uh_ok = jnp.where(uh < POOL_HALO, 1 - first.astype(jnp.int32), 1 - last.astype(jnp.int32)) > 0
    rcol = lax.broadcasted_iota(jnp.int32, (tm, 1), 0)
    seq_lo = jnp.where(first, 0, -POOL_HALO)
    seq_hi = jnp.where(last, tm - 1, tm - 1 + POOL_HALO)

    p = p_ref[...]
    halo = jnp.concatenate([pprev_ref[...], pnext_ref[...]], axis=0)

    def split(x):
        hi = x.astype(BF16)
        return hi, (x - hi.astype(F32)).astype(BF16)

    p_hi, p_lo = split(p)
    h_hi, h_lo = split(halo)
    for gi, win in enumerate(POOL_WINDOWS):
        sl = slice(gi * POOL_GROUP, (gi + 1) * POOL_GROUP)
        lo_off, hi_off = win // 2, win - 1 - win // 2
        band_c = jnp.where((u >= r - lo_off) & (u <= r + hi_off), 1.0, 0.0).astype(BF16)
        band_h = jnp.where((uh_pos >= rh - lo_off) & (uh_pos <= rh + hi_off) & uh_ok,
                           1.0, 0.0).astype(BF16)
        tot = (jnp.dot(band_c, p_hi[:, sl], preferred_element_type=F32)
               + jnp.dot(band_c, p_lo[:, sl], preferred_element_type=F32)
               + jnp.dot(band_h, h_hi[:, sl], preferred_element_type=F32)
               + jnp.dot(band_h, h_lo[:, sl], preferred_element_type=F32))
        lo = jnp.maximum(rcol - lo_off, seq_lo)
        hi = jnp.minimum(rcol + hi_off, seq_hi)
        cnt = (hi - lo + 1).astype(F32)
        dlt = tot / cnt - p[:, sl]
        y = jnp.dot(dlt.astype(BF16), pw_ref[gi], preferred_element_type=F32)
        yp_ref[:, sl] = (y * ps_ref[:, sl]).astype(BF16)

    uu = _gelu_tanh(u_ref[...])
    vv = _gelu_tanh(v_ref[...])
    mu = jnp.mean(vv, axis=-1, keepdims=True)
    vc = vv - mu
    vn = (vc * lax.rsqrt(jnp.mean(vc * vc, axis=-1, keepdims=True) + EPS) * ng_ref[...]).astype(BF16)
    for c in range(tm // GMLP_CHUNK):
        rows = slice(c * GMLP_CHUNK, (c + 1) * GMLP_CHUNK)
        for g in range(GMLP_GROUPS):
            sl = slice(g * GMLP_CHUNK, (g + 1) * GMLP_CHUNK)
            sv = jnp.dot(ws_ref[g], vn[rows, sl], preferred_element_type=F32) + bst_ref[:, g:g + 1]
            yg_ref[rows, sl] = (uu[rows, sl] * sv).astype(BF16)


def _mixers(proj, pool_w_bf16, pool_scale, gmlp_norm_g, gmlp_ws_bf16, gmlp_bs, *, t_rows,
            batch, n, lc):
    tm = ROW_TILE
    nt, nc = n // tm, lc // tm
    n_lat = batch * nt
    hb = tm // POOL_HALO
    n_hblocks = proj.shape[0] // POOL_HALO
    pool_cb, gu_cb, gv_cb = 3 * ATTN_WIDTH // POOL_WIDTH, 3 * ATTN_WIDTH // POOL_WIDTH + 1, \
        3 * ATTN_WIDTH // POOL_WIDTH + 2
    kernel = functools.partial(_mixers_kernel, tiles_per_seq=(nt, n_lat, nc))
    return pl.pallas_call(
        kernel,
        out_shape=(jax.ShapeDtypeStruct((t_rows, POOL_WIDTH), BF16),
                   jax.ShapeDtypeStruct((t_rows, GMLP_WIDTH), BF16)),
        grid=(t_rows // tm,),
        in_specs=[pl.BlockSpec((tm, POOL_WIDTH), lambda i: (i, pool_cb)),
                  pl.BlockSpec((POOL_HALO, POOL_WIDTH),
                               lambda i: (jnp.maximum(i * hb - 1, 0), pool_cb)),
                  pl.BlockSpec((POOL_HALO, POOL_WIDTH),
                               lambda i: (jnp.minimum((i + 1) * hb, n_hblocks - 1), pool_cb)),
                  pl.BlockSpec((tm, GMLP_WIDTH), lambda i: (i, gu_cb)),
                  pl.BlockSpec((tm, GMLP_WIDTH), lambda i: (i, gv_cb)),
                  pl.BlockSpec((len(POOL_WINDOWS), POOL_GROUP, POOL_GROUP), lambda i: (0, 0, 0)),
                  pl.BlockSpec((1, POOL_WIDTH), lambda i: (0, 0)),
                  pl.BlockSpec((1, GMLP_WIDTH), lambda i: (0, 0)),
                  pl.BlockSpec((GMLP_GROUPS, GMLP_CHUNK, GMLP_CHUNK), lambda i: (0, 0, 0)),
                  pl.BlockSpec((GMLP_CHUNK, GMLP_GROUPS), lambda i: (0, 0))],
        out_specs=(pl.BlockSpec((tm, POOL_WIDTH), lambda i: (i, 0)),
                   pl.BlockSpec((tm, GMLP_WIDTH), lambda i: (i, 0))),
        compiler_params=_cparams(("arbitrary",)),
        name="local_mixers",
    )(proj, proj, proj, proj, proj, pool_w_bf16, pool_scale.reshape(1, POOL_WIDTH),
      gmlp_norm_g.reshape(1, GMLP_WIDTH), gmlp_ws_bf16, gmlp_bs.T)


def _outproj_kernel(ya_ref, yp_ref, yg_ref, w_ref, x_ref, mod_ref, o_ref):
    acc = jnp.dot(ya_ref[...], w_ref[0:ATTN_WIDTH, :], preferred_element_type=F32)
    acc = acc + jnp.dot(yp_ref[...], w_ref[ATTN_WIDTH:ATTN_WIDTH + POOL_WIDTH, :],
                        preferred_element_type=F32)
    acc = acc + jnp.dot(yg_ref[...], w_ref[ATTN_WIDTH + POOL_WIDTH:, :],
                        preferred_element_type=F32)
    o_ref[...] = x_ref[...] + mod_ref[0, 2:3, :] * acc


def _outproj(y_attn, y_pool, y_gmlp, w_bf16, x, mod, seg_of_tile, *, t_rows):
    d = x.shape[1]
    tm = MM_ROW_TILE
    return pl.pallas_call(
        _outproj_kernel,
        out_shape=jax.ShapeDtypeStruct((t_rows, d), F32),
        grid=(t_rows // tm,),
        in_specs=[pl.BlockSpec((tm, ATTN_WIDTH), lambda i: (i, 0)),
                  pl.BlockSpec((tm, POOL_WIDTH), lambda i: (i, 0)),
                  pl.BlockSpec((tm, GMLP_WIDTH), lambda i: (i, 0)),
                  pl.BlockSpec(w_bf16.shape, lambda i: (0, 0)),
                  pl.BlockSpec((tm, d), lambda i: (i, 0)),
                  pl.BlockSpec((1, 6, d), lambda i: (seg_of_tile(i, tm), 0, 0))],
        out_specs=pl.BlockSpec((tm, d), lambda i: (i, 0)),
        compiler_params=_cparams(("arbitrary",)),
        name="out_proj",
    )(y_attn, y_pool, y_gmlp, w_bf16, x, mod)


def _router_kernel(x_ref, g_ref, mod_ref, wgh_ref, wgl_ref, weh_ref, wel_ref, bg_ref, be_ref,
                   h_ref, route_ref, cnt_ref, carry_ref):
    tm = x_ref.shape[0]

    @pl.when(pl.program_id(0) == 0)
    def _():
        carry_ref[...] = jnp.zeros(carry_ref.shape, F32)

    x = x_ref[...]
    ms = jnp.mean(x * x, axis=-1, keepdims=True)
    y = x * lax.rsqrt(ms + EPS) * g_ref[...]
    h = y * (1.0 + mod_ref[0, 4:5, :]) + mod_ref[0, 3:4, :]
    h_ref[...] = h

    hh = h.astype(BF16)
    hl = (h - hh.astype(F32)).astype(BF16)

    def logits(w_hi, w_lo, b):
        return (jnp.dot(hh, w_hi[...], preferred_element_type=F32)
                + jnp.dot(hl, w_hi[...], preferred_element_type=F32)
                + jnp.dot(hh, w_lo[...], preferred_element_type=F32) + b[...])

    gl = logits(wgh_ref, wgl_ref, bg_ref)
    el = logits(weh_ref, wel_ref, be_ref)
    lane = lax.broadcasted_iota(jnp.int32, (tm, LANES), 1).astype(F32)
    far = float(LANES)

    def first_argmax(v):
        vmax = jnp.max(v, axis=-1, keepdims=True)
        idx = jnp.min(jnp.where(v == vmax, lane, far), axis=-1, keepdims=True)
        return vmax, idx

    glm = jnp.where(lane < N_GROUPS, gl, NEG_BIG)
    gmax, g_idx = first_argmax(glm)
    g_val = 1.0 / jnp.sum(jnp.exp(glm - gmax), axis=-1, keepdims=True)
    in_grp = (jnp.floor(lane * (1.0 / EXPERTS_PER_GROUP)) == g_idx) & (lane < N_EXPERTS)
    elm = jnp.where(in_grp, el, NEG_BIG)
    m1, e1 = first_argmax(elm)
    elm2 = jnp.where(lane == e1, NEG_BIG, elm)
    m2, e2 = first_argmax(elm2)
    rr = jnp.exp(m2 - m1)
    w1 = g_val / (1.0 + rr)
    w2 = g_val * rr / (1.0 + rr)

    sel1 = lane == e1
    sel2 = lane == e2
    esum = jnp.where(sel1 | sel2, 1.0, 0.0)
    rI = lax.broadcasted_iota(jnp.int32, (tm, tm), 0)
    cI = lax.broadcasted_iota(jnp.int32, (tm, tm), 1)
    ltri = jnp.where(rI > cI, 1.0, 0.0).astype(BF16)
    before = jnp.dot(ltri, esum.astype(BF16), preferred_element_type=F32) + carry_ref[0:1, :]
    rank1 = jnp.sum(jnp.where(sel1, before, 0.0), axis=-1, keepdims=True)
    rank2 = jnp.sum(jnp.where(sel2, before, 0.0), axis=-1, keepdims=True)
    carry_ref[0:1, :] = carry_ref[0:1, :] + jnp.sum(esum, axis=0, keepdims=True)
    cnt_ref[...] = carry_ref[...]

    out = jnp.where(lane == 0, e1, 0.0)
    out = jnp.where(lane == 1, e2, out)
    out = jnp.where(lane == 2, rank1, out)
    out = jnp.where(lane == 3, rank2, out)
    out = jnp.where(lane == 4, w1, out)
    out = jnp.where(lane == 5, w2, out)
    route_ref[...] = out[:, 0:8]


def _router(x, norm_g, mod, rg_w, rg_b, re_w, re_b, seg_of_tile, *, t_rows):
    d = x.shape[1]
    tm = ROW_TILE

    def pad_split(w):
        wp = jnp.zeros((d, LANES), F32).at[:, :w.shape[1]].set(w)
        hi = wp.astype(BF16)
        return hi, (wp - hi.astype(F32)).astype(BF16)

    def pad_bias(b):
        return jnp.zeros((1, LANES), F32).at[0, :b.shape[0]].set(b)

    wgh, wgl = pad_split(rg_w)
    weh, wel = pad_split(re_w)
    wspec = pl.BlockSpec((d, LANES), lambda i: (0, 0))
    bspec = pl.BlockSpec((1, LANES), lambda i: (0, 0))
    return pl.pallas_call(
        _router_kernel,
        out_shape=(jax.ShapeDtypeStruct((t_rows, d), F32),
                   jax.ShapeDtypeStruct((t_rows, 8), F32),
                   jax.ShapeDtypeStruct((8, LANES), F32)),
        grid=(t_rows // tm,),
        in_specs=[pl.BlockSpec((tm, d), lambda i: (i, 0)),
                  pl.BlockSpec((1, d), lambda i: (0, 0)),
                  pl.BlockSpec((1, 6, d), lambda i: (seg_of_tile(i, tm), 0, 0)),
                  wspec, wspec, wspec, wspec, bspec, bspec],
        out_specs=(pl.BlockSpec((tm, d), lambda i: (i, 0)),
                   pl.BlockSpec((tm, 8), lambda i: (i, 0)),
                   pl.BlockSpec((8, LANES), lambda i: (0, 0))),
        scratch_shapes=[pltpu.VMEM((8, LANES), F32)],
        compiler_params=_cparams(("arbitrary",)),
        name="moe_router",
    )(x, norm_g.reshape(1, d), mod, wgh, wgl, weh, wel, pad_bias(rg_b), pad_bias(re_b))


def _dispatch_kernel(dest_ref, h_ref, xs_in_ref, xs_ref, sem):
    del xs_in_ref
    tm = h_ref.shape[0]
    base = pl.program_id(0) * (2 * tm)

    def row_copy(r, slot):
        return pltpu.make_async_copy(h_ref.at[pl.ds(r, 1)], xs_ref.at[pl.ds(slot, 1)], sem)

    def issue(r, carry):
        row_copy(r, dest_ref[base + 2 * r]).start()
        row_copy(r, dest_ref[base + 2 * r + 1]).start()
        return carry

    def drain(r, carry):
        row_copy(0, 0).wait()
        row_copy(0, 0).wait()
        return carry

    lax.fori_loop(0, tm, issue, 0)
    lax.fori_loop(0, tm, drain, 0)


def _dispatch(dest_flat, h2, n_slots):
    t, d = h2.shape
    tm = ROW_TILE
    return pl.pallas_call(
        _dispatch_kernel,
        out_shape=jax.ShapeDtypeStruct((n_slots, d), F32),
        grid_spec=pltpu.PrefetchScalarGridSpec(
            num_scalar_prefetch=1,
            grid=(t // tm,),
            in_specs=[pl.BlockSpec((tm, d), lambda i, dest: (i, 0)),
                      pl.BlockSpec(memory_space=pl.ANY)],
            out_specs=pl.BlockSpec(memory_space=pl.ANY),
            scratch_shapes=[pltpu.SemaphoreType.DMA]),
        input_output_aliases={2: 0},
        compiler_params=_cparams(("arbitrary",)),
        name="moe_dispatch",
    )(dest_flat, h2, jnp.zeros((n_slots, d), F32))


def _expert_kernel(blk_e_ref, n_used_ref, xs_ref, wg_ref, wu_ref, wd_ref, ys_ref):
    del blk_e_ref
    i = pl.program_id(0)

    @pl.when(i < n_used_ref[0])
    def _():
        xb = xs_ref[...].astype(BF16)
        a = jnp.dot(xb, wg_ref[0], preferred_element_type=F32)
        b = jnp.dot(xb, wu_ref[0], preferred_element_type=F32)
        hmid = (a * jax.nn.sigmoid(a) * b).astype(BF16)
        ys_ref[...] = jnp.dot(hmid, wd_ref[0], preferred_element_type=F32)

    @pl.when(i >= n_used_ref[0])
    def _():
        ys_ref[...] = jnp.zeros(ys_ref.shape, F32)


def _experts(blk_e, n_used, xs, wg_bf16, wu_bf16, wd_bf16):
    n_slots, d = xs.shape
    ff = wg_bf16.shape[2]
    return pl.pallas_call(
        _expert_kernel,
        out_shape=jax.ShapeDtypeStruct((n_slots, d), F32),
        grid_spec=pltpu.PrefetchScalarGridSpec(
            num_scalar_prefetch=2,
            grid=(n_slots // MOE_BLOCK,),
            in_specs=[pl.BlockSpec((MOE_BLOCK, d), lambda i, be, nu: (i, 0)),
                      pl.BlockSpec((1, d, ff), lambda i, be, nu: (be[i], 0, 0)),
                      pl.BlockSpec((1, d, ff), lambda i, be, nu: (be[i], 0, 0)),
                      pl.BlockSpec((1, ff, d), lambda i, be, nu: (be[i], 0, 0))],
            out_specs=pl.BlockSpec((MOE_BLOCK, d), lambda i, be, nu: (i, 0))),
        compiler_params=_cparams(("arbitrary",)),
        name="moe_experts",
    )(blk_e, n_used, xs, wg_bf16, wu_bf16, wd_bf16)


def _combine_kernel(dest_ref, x_ref, mod_ref, route_ref, ys_ref, o_ref, y0_ref, y1_ref, sem):
    tm = x_ref.shape[0]
    base = pl.program_id(0) * (2 * tm)

    def row_copy(slot, buf, r):
        return pltpu.make_async_copy(ys_ref.at[pl.ds(slot, 1)], buf.at[pl.ds(r, 1)], sem)

    def issue(r, carry):
        row_copy(dest_ref[base + 2 * r], y0_ref, r).start()
        row_copy(dest_ref[base + 2 * r + 1], y1_ref, r).start()
        return carry

    def drain(r, carry):
        row_copy(0, y0_ref, 0).wait()
        row_copy(0, y1_ref, 0).wait()
        return carry

    lax.fori_loop(0, tm, issue, 0)
    lax.fori_loop(0, tm, drain, 0)
    f = route_ref[:, 4:5] * y0_ref[...] + route_ref[:, 5:6] * y1_ref[...]
    o_ref[...] = x_ref[...] + mod_ref[0, 5:6, :] * f


def _combine(dest_flat, x, mod, route, ys, seg_of_tile):
    t, d = x.shape
    tm = ROW_TILE
    return pl.pallas_call(
        _combine_kernel,
        out_shape=jax.ShapeDtypeStruct((t, d), F32),
        grid_spec=pltpu.PrefetchScalarGridSpec(
            num_scalar_prefetch=1,
            grid=(t // tm,),
            in_specs=[pl.BlockSpec((tm, d), lambda i, dest: (i, 0)),
                      pl.BlockSpec((1, 6, d), lambda i, dest: (seg_of_tile(i, tm), 0, 0)),
                      pl.BlockSpec((tm, 8), lambda i, dest: (i, 0)),
                      pl.BlockSpec(memory_space=pl.ANY)],
            out_specs=pl.BlockSpec((tm, d), lambda i, dest: (i, 0)),
            scratch_shapes=[pltpu.VMEM((tm, d), F32), pltpu.VMEM((tm, d), F32),
                            pltpu.SemaphoreType.DMA]),
        compiler_params=_cparams(("arbitrary",)),
        name="moe_combine",
    )(dest_flat, x, mod, route, ys)


def _moe(x, norm_g, mod, rg_w, rg_b, re_w, re_b, wg_bf16, wu_bf16, wd_bf16, seg_of_tile):
    t = x.shape[0]
    h2, route, cnt = _router(x, norm_g, mod, rg_w, rg_b, re_w, re_b, seg_of_tile, t_rows=t)
    counts = cnt[0, :N_EXPERTS].astype(jnp.int32)
    padded = (counts + MOE_BLOCK - 1) // MOE_BLOCK * MOE_BLOCK
    pend = jnp.cumsum(padded)
    pstart = pend - padded
    n_slots = -(-(2 * t + N_EXPERTS * (MOE_BLOCK - 1)) // MOE_BLOCK) * MOE_BLOCK
    n_blk = n_slots // MOE_BLOCK
    eid = route[:, 0:2].astype(jnp.int32)
    rank = route[:, 2:4].astype(jnp.int32)
    dest = (pstart[eid] + rank).reshape(-1)
    blk_e = jnp.minimum(jnp.searchsorted(pend, jnp.arange(n_blk, dtype=jnp.int32) * MOE_BLOCK,
                                         side='right'), N_EXPERTS - 1).astype(jnp.int32)
    n_used = (pend[-1] // MOE_BLOCK).astype(jnp.int32).reshape(1)
    xs = _dispatch(dest, h2, n_slots)
    ys = _experts(blk_e, n_used, xs, wg_bf16, wu_bf16, wd_bf16)
    return _combine(dest, x, mod, route, ys, seg_of_tile)


def kernel(x, c, ctx, c_ctx, ada_w, ada_b, norm1_g, norm2_g, w_in, q_norm_g, k_norm_g, lam_q1, lam_k1, lam_q2, lam_k2, subln_g, pool_w, pool_scale, gmlp_norm_g, gmlp_ws, gmlp_bs, w_out, router_g_w, router_g_b, router_e_w, router_e_b, w_gate, w_up, w_down):
    batch, n, d = x.shape
    lc = ctx.shape[1]
    depth = ada_w.shape[0]
    t_lat = batch * n
    t_all = t_lat + batch * lc
    assert n % MM_ROW_TILE == 0 and (batch * lc) % MM_ROW_TILE == 0 and lc % KEY_BLOCK == 0
    assert n % Q_TILE == 0 and lc % Q_TILE == 0 and n % GRID_W == 0

    def seg_of_tile(i, tm):
        return jnp.minimum(i // (n // tm), batch)

    xs = jnp.concatenate([x.reshape(t_lat, d), ctx.reshape(batch * lc, d)], axis=0)
    c_all = jnp.concatenate([c, c_ctx[None, :]], axis=0)
    cos_t, sin_t = _rope_tables(n, KEY_BLOCK)

    for l in range(depth):
        last = l == depth - 1
        lam_init = 0.8 - 0.6 * math.exp(-0.3 * l)
        lam = (jnp.exp(jnp.sum(lam_q1[l] * lam_k1[l])) - jnp.exp(jnp.sum(lam_q2[l] * lam_k2[l]))
               + lam_init).astype(F32)
        mod = _ada(c_all, ada_w[l], ada_b[l])
        proj = _inproj(xs, norm1_g[l], mod, w_in[l].astype(BF16), seg_of_tile)
        q_all, k_all, vt_all = _prep(proj, cos_t, sin_t,
                                     jnp.tile(q_norm_g[l], 2).reshape(1, LANES),
                                     jnp.tile(k_norm_g[l], 2).reshape(1, LANES),
                                     batch=batch, n=n, lc=lc)
        t_rows = t_lat if last else t_all
        y_attn = _attention(lam, q_all, k_all, vt_all, subln_g[l], batch=batch,
                            q_row0=0, lq=n, lk=lc + n, lam_init=lam_init)
        if not last:
            y_ctx = _attention(lam, q_all, k_all, vt_all, subln_g[l], batch=batch,
                               q_row0=t_lat, lq=lc, lk=lc, lam_init=lam_init)
            y_attn = jnp.concatenate([y_attn, y_ctx], axis=0)
        y_pool, y_gmlp = _mixers(proj, pool_w[l].astype(BF16), pool_scale[l], gmlp_norm_g[l],
                                 gmlp_ws[l].astype(BF16), gmlp_bs[l], t_rows=t_rows,
                                 batch=batch, n=n, lc=lc)
        x1 = _outproj(y_attn, y_pool, y_gmlp, w_out[l].astype(BF16), xs, mod, seg_of_tile,
                      t_rows=t_rows)
        xs = _moe(x1, norm2_g[l], mod, router_g_w[l], router_g_b[l], router_e_w[l],
                  router_e_b[l], w_gate[l].astype(BF16), w_up[l].astype(BF16),
                  w_down[l].astype(BF16), seg_of_tile)
    return xs[:t_lat].reshape(batch, n, d)
```

```python
import functools
import math

import jax
import jax.numpy as jnp
from jax import lax
from jax.experimental import pallas as pl
from jax.experimental.pallas import tpu as pltpu

F32 = jnp.float32
BF16 = jnp.bfloat16

EPS = 1e-6
GRID_W = 64
ROPE_BASE = 10000.0
N_HEADS = 8
HEAD_DIM = 128
QK_DIM = 64
ROPE_HALF = 16
ATTN_WIDTH = N_HEADS * HEAD_DIM
POOL_WINDOWS = (2, 4, 8, 16)
POOL_GROUP = 128
POOL_WIDTH = 512
POOL_HALO = 8
GMLP_WIDTH = 512
GMLP_GROUPS = 4
GMLP_CHUNK = 128
N_GROUPS = 4
EXPERTS_PER_GROUP = 8
N_EXPERTS = 32
MOE_BLOCK = 256
KEY_BLOCK = 256
FAST_SKEW = 2
FAST_GROUP_MAX = 65
FAST_MIN_DENOM = 2.0 ** -40
FAST_SHIFT_FRACTION = 0.2
FAST_MAX_EXPONENT = 100.0
MAX_KB_PER_TILE = 5
VT_ROWS = HEAD_DIM + 16
LANES = 128
NEG_BIG = -1e30
Q_SCALE_LOG2 = QK_DIM ** -0.5 * math.log2(math.e)

VMEM_LIMIT_V7X = 56 * 1024 * 1024

ROW_TILE = 256
MM_ROW_TILE = 512
Q_TILE = 256


def _cparams(semantics, vmem=VMEM_LIMIT_V7X, flags=None):
    return pltpu.CompilerParams(dimension_semantics=semantics, vmem_limit_bytes=vmem, flags=flags)


def _gelu_tanh(x):
    return 0.5 * x * (1.0 + jnp.tanh(math.sqrt(2.0 / math.pi) * (x + 0.044715 * x * x * x)))


def _ada_kernel(ct_ref, w_ref, b_ref, o_ref, *, nseg):
    ct = ct_ref[...]
    at = ct * jax.nn.sigmoid(ct)
    d = w_ref.shape[0]
    chunk = 256
    accs = [jnp.zeros((1, w_ref.shape[1]), F32) for _ in range(nseg)]
    for kc in range(d // chunk):
        w = w_ref[kc * chunk:(kc + 1) * chunk, :]
        for r in range(nseg):
            accs[r] = accs[r] + jnp.sum(at[kc * chunk:(kc + 1) * chunk, r:r + 1] * w,
                                        axis=0, keepdims=True)
    o_ref[...] = jnp.zeros(o_ref.shape, F32)
    for r in range(nseg):
        o_ref[r:r + 1, :] = accs[r] + b_ref[...]


def _ada(c_all, ada_w, ada_b, layer):
    nseg, d = c_all.shape
    n_out = ada_w.shape[2]
    tn = 768
    ct = jnp.zeros((d, 8), F32).at[:, :nseg].set(c_all.T)
    out = pl.pallas_call(
        functools.partial(_ada_kernel, nseg=nseg),
        out_shape=jax.ShapeDtypeStruct((8, n_out), F32),
        grid=(n_out // tn,),
        in_specs=[pl.BlockSpec((d, 8), lambda j: (0, 0)),
                  pl.BlockSpec((None, d, tn), lambda j: (layer, 0, j)),
                  pl.BlockSpec((1, tn), lambda j: (0, j))],
        out_specs=pl.BlockSpec((8, tn), lambda j: (0, j)),
        compiler_params=_cparams(("arbitrary",)),
        name="ada_mod",
    )(ct, ada_w, ada_b.reshape(1, n_out))
    return out[:nseg].reshape(nseg, 6, d)


def _inproj_kernel(x_ref, g_ref, mod_ref, w_ref, o_ref, h_ref):
    @pl.when(pl.program_id(1) == 0)
    def _():
        x = x_ref[...]
        ms = jnp.mean(x * x, axis=-1, keepdims=True)
        y = x * lax.rsqrt(ms + EPS) * g_ref[...]
        h_ref[...] = (y * (1.0 + mod_ref[0, 1:2, :]) + mod_ref[0, 0:1, :]).astype(BF16)

    o_ref[...] = jnp.dot(h_ref[...], w_ref[...], preferred_element_type=F32)


def _inproj(x, norm_g, mod, w_bf16, seg_of_tile):
    t, d = x.shape
    n_out = w_bf16.shape[1]
    tm, tn = MM_ROW_TILE, 1536
    return pl.pallas_call(
        _inproj_kernel,
        out_shape=jax.ShapeDtypeStruct((t, n_out), F32),
        grid=(t // tm, n_out // tn),
        in_specs=[pl.BlockSpec((tm, d), lambda i, j: (i, 0)),
                  pl.BlockSpec((1, d), lambda i, j: (0, 0)),
                  pl.BlockSpec((1, 6, d), lambda i, j: (seg_of_tile(i, tm), 0, 0)),
                  pl.BlockSpec((d, tn), lambda i, j: (0, j))],
        out_specs=pl.BlockSpec((tm, tn), lambda i, j: (i, j)),
        scratch_shapes=[pltpu.VMEM((tm, d), BF16)],
        compiler_params=_cparams(("arbitrary", "arbitrary")),
        name="in_proj",
    )(x, norm_g.reshape(1, d), mod, w_bf16)


def _prep_kernel(q_ref, k_ref, v_ref, cos_ref, sin_ref, qg_ref, kg_ref,
                 qo_ref, ko_ref, vo_ref):
    tm = q_ref.shape[0]
    cos = cos_ref[...]
    sin = sin_ref[...]
    lane = lax.broadcasted_iota(jnp.int32, (tm, LANES), 1)
    first_half = (lane % (2 * ROPE_HALF)) < ROPE_HALF
    r_i = lax.broadcasted_iota(jnp.int32, (LANES, LANES), 0) // QK_DIM
    c_i = lax.broadcasted_iota(jnp.int32, (LANES, LANES), 1) // QK_DIM
    group_ones = jnp.where(r_i == c_i, 1.0, 0.0).astype(BF16)

    def norm_rope(x, g, scale):
        sq = x * x
        hi = sq.astype(BF16)
        lo = (sq - hi.astype(F32)).astype(BF16)
        ss = (jnp.dot(hi, group_ones, preferred_element_type=F32)
              + jnp.dot(lo, group_ones, preferred_element_type=F32))
        y = x * lax.rsqrt(ss * (1.0 / QK_DIM) + EPS) * g
        partner = jnp.where(first_half,
                            pltpu.roll(y, LANES - ROPE_HALF, 1),
                            pltpu.roll(y, ROPE_HALF, 1))
        return (y * cos + partner * sin) * scale

    qg = qg_ref[...]
    kg = kg_ref[...]
    pad_row = lax.broadcasted_iota(jnp.int32, (VT_ROWS - HEAD_DIM, tm), 0)
    ones_rows = jnp.where(pad_row == 0, 1.0, 0.0).astype(BF16)
    for h in range(N_HEADS):
        sl = slice(h * HEAD_DIM, (h + 1) * HEAD_DIM)
        qo_ref[:, sl] = norm_rope(q_ref[:, sl], qg, Q_SCALE_LOG2).astype(BF16)
        ko_ref[0, :, sl] = norm_rope(k_ref[:, sl], kg, 1.0).astype(BF16)
        vo_ref[0, 0, h * VT_ROWS:h * VT_ROWS + HEAD_DIM, :] = v_ref[:, sl].T.astype(BF16)
        vo_ref[0, 0, h * VT_ROWS + HEAD_DIM:(h + 1) * VT_ROWS, :] = ones_rows


def _prep(proj, cos_t, sin_t, qg, kg, *, batch, n, lc):
    t = proj.shape[0]
    tm = KEY_BLOCK
    nt, nc = n // tm, lc // tm
    n_lat_tiles = batch * nt
    lk = lc + n
    nkb = lk // KEY_BLOCK

    def kv_block(i):
        lat = i < n_lat_tiles
        ic = i - n_lat_tiles
        b = jnp.where(lat, i // nt, ic // nc)
        kb = jnp.where(lat, nc + i % nt, ic % nc)
        return b, kb

    def tab_block(i):
        return jnp.where(i < n_lat_tiles, i % nt, nt)

    return pl.pallas_call(
        _prep_kernel,
        out_shape=(jax.ShapeDtypeStruct((t, ATTN_WIDTH), BF16),
                   jax.ShapeDtypeStruct((batch, lk, ATTN_WIDTH), BF16),
                   jax.ShapeDtypeStruct((batch, nkb, N_HEADS * VT_ROWS, KEY_BLOCK), BF16)),
        grid=(t // tm,),
        in_specs=[pl.BlockSpec((tm, ATTN_WIDTH), lambda i: (i, 0)),
                  pl.BlockSpec((tm, ATTN_WIDTH), lambda i: (i, 1)),
                  pl.BlockSpec((tm, ATTN_WIDTH), lambda i: (i, 2)),
                  pl.BlockSpec((tm, LANES), lambda i: (tab_block(i), 0)),
                  pl.BlockSpec((tm, LANES), lambda i: (tab_block(i), 0)),
                  pl.BlockSpec((1, LANES), lambda i: (0, 0)),
                  pl.BlockSpec((1, LANES), lambda i: (0, 0))],
        out_specs=(pl.BlockSpec((tm, ATTN_WIDTH), lambda i: (i, 0)),
                   pl.BlockSpec((1, tm, ATTN_WIDTH), lambda i: kv_block(i) + (0,)),
                   pl.BlockSpec((1, 1, N_HEADS * VT_ROWS, KEY_BLOCK),
                                lambda i: kv_block(i) + (0, 0))),
        compiler_params=_cparams(("arbitrary",)),
        name="qkv_prep",
    )(proj, proj, proj, cos_t, sin_t, qg, kg)


def _rope_tables(n, tm):
    pos = jnp.arange(n, dtype=jnp.int32)
    row = (pos // GRID_W).astype(F32)
    col = (pos % GRID_W).astype(F32)
    inv = ROPE_BASE ** (-jnp.arange(ROPE_HALF, dtype=F32) / ROPE_HALF)
    ang_r = row[:, None] * inv[None, :]
    ang_c = col[:, None] * inv[None, :]
    ang = jnp.concatenate([ang_r, ang_r, ang_c, ang_c], axis=-1)
    sign = jnp.tile(jnp.concatenate([-jnp.ones(ROPE_HALF, F32), jnp.ones(ROPE_HALF, F32)]), 2)
    cos = jnp.tile(jnp.cos(ang), (1, 2))
    sin = jnp.tile(jnp.sin(ang) * sign[None, :], (1, 2))
    cos = jnp.concatenate([cos, jnp.ones((tm, LANES), F32)], axis=0)
    sin = jnp.concatenate([sin, jnp.zeros((tm, LANES), F32)], axis=0)
    return cos, sin


NT_DIMS = (((1,), (1,)), ((), ()))


def _attn_online_pass(k_ref, vt_ref, qbd, acc_ref, m_ref, *, n_tiles, kb_per_tile):
    tk = kb_per_tile * KEY_BLOCK
    m_ref[...] = jnp.full(m_ref.shape, NEG_BIG, F32)
    acc_ref[...] = jnp.zeros(acc_ref.shape, F32)

    def tile(t, carry):
        kt = k_ref[0, pl.ds(pl.multiple_of(t * tk, tk), tk), :]
        s = lax.dot_general(kt, qbd, NT_DIMS, preferred_element_type=F32)
        m_old = m_ref[...]
        m_new = jnp.maximum(m_old, jnp.max(s, axis=0, keepdims=True))
        p = jnp.exp2((s - m_new).astype(BF16))
        vt = jnp.concatenate([vt_ref[0, t * kb_per_tile + c] for c in range(kb_per_tile)], axis=1)
        acc_ref[...] = (jnp.exp2(m_old - m_new) * acc_ref[...]
                        + jnp.dot(vt, p, preferred_element_type=F32))
        m_ref[...] = m_new
        return carry

    lax.fori_loop(0, n_tiles, tile, 0)


def _attn_kernel(par_ref, q_ref, k_ref, vt_ref, g_ref, o_ref, acc_ref, m_ref,
                 *, n_blocks, kb_per_tile, out_scale):
    tq = q_ref.shape[0]
    q = q_ref[...]
    lane = lax.broadcasted_iota(jnp.int32, (tq, HEAD_DIM), 1)
    zero = jnp.zeros_like(q)
    qbd = jnp.concatenate([jnp.where(lane < QK_DIM, q, zero),
                           jnp.where(lane >= QK_DIM, q, zero)], axis=0)

    qf = q.astype(F32)
    qsq = qf * qf
    qsq_hi = qsq.astype(BF16)
    qsq_lo = (qsq - qsq_hi.astype(F32)).astype(BF16)
    sel_r = lax.broadcasted_iota(jnp.int32, (8, HEAD_DIM), 0)
    sel_l = lax.broadcasted_iota(jnp.int32, (8, HEAD_DIM), 1)
    sel = jnp.where(sel_l // QK_DIM == sel_r, 1.0, 0.0).astype(BF16)
    qn2 = (lax.dot_general(sel, qsq_hi, NT_DIMS, preferred_element_type=F32)
           + lax.dot_general(sel, qsq_lo, NT_DIMS, preferred_element_type=F32))
    k_bound = par_ref[1]
    bound = jnp.concatenate([jnp.sqrt(qn2[0:1]) * k_bound, jnp.sqrt(qn2[1:2]) * k_bound],
                            axis=1)
    shift = jnp.maximum(FAST_SHIFT_FRACTION * bound, bound - FAST_MAX_EXPONENT)
    acc_ref[...] = jnp.zeros(acc_ref.shape, F32)
    group = max(c for c in range(1, FAST_GROUP_MAX + 1) if n_blocks % c == 0)

    def fast_group(u, carry):
        pv = None
        scores = {}
        for cc in range(group + FAST_SKEW):
            if cc < group:
                row0 = pl.multiple_of((u * group + cc) * KEY_BLOCK, KEY_BLOCK)
                kt = k_ref[0, pl.ds(row0, KEY_BLOCK), :]
                scores[cc] = lax.dot_general(kt, qbd, NT_DIMS, preferred_element_type=F32)
            c = cc - FAST_SKEW
            if c >= 0:
                p = jnp.exp2((scores.pop(c) - shift).astype(BF16))
                d = jnp.dot(vt_ref[0, u * group + c], p, preferred_element_type=F32)
                pv = d if pv is None else pv + d
        acc_ref[...] += pv
        return carry

    if n_blocks == group:
        fast_group(0, 0)
    else:
        lax.fori_loop(0, n_blocks // group, fast_group, 0)
    denom_min = jnp.min(acc_ref[HEAD_DIM:HEAD_DIM + 1, :])

    @pl.when(jnp.logical_not(denom_min >= FAST_MIN_DENOM))
    def _():
        _attn_online_pass(k_ref, vt_ref, qbd, acc_ref, m_ref,
                          n_tiles=n_blocks // kb_per_tile, kb_per_tile=kb_per_tile)

    o = acc_ref[0:HEAD_DIM, :] / acc_ref[HEAD_DIM:HEAD_DIM + 1, :]
    d = o[:, :tq] - par_ref[0] * o[:, tq:]
    ms = jnp.mean(d * d, axis=0, keepdims=True)
    y = d * lax.rsqrt(ms + EPS) * (g_ref[...] * out_scale)
    o_ref[...] = y.T.astype(BF16)


def _attention(lam, k_bound, q_all, k_all, vt_all, subln_g, *, batch, q_row0, lq, lk, lam_init):
    tq = Q_TILE
    nq = lq // tq
    nkb = lk // KEY_BLOCK
    kb_per_tile = max(c for c in range(1, MAX_KB_PER_TILE + 1) if nkb % c == 0)
    qb0 = q_row0 // tq
    kernel = functools.partial(_attn_kernel, n_blocks=nkb, kb_per_tile=kb_per_tile,
                               out_scale=1.0 - lam_init)
    return pl.pallas_call(
        kernel,
        out_shape=jax.ShapeDtypeStruct((batch * lq, ATTN_WIDTH), BF16),
        grid=(batch, N_HEADS, nq),
        in_specs=[pl.BlockSpec(memory_space=pltpu.SMEM),
                  pl.BlockSpec((tq, HEAD_DIM), lambda b, h, qi: (qb0 + b * nq + qi, h)),
                  pl.BlockSpec((1, lk, HEAD_DIM), lambda b, h, qi: (b, 0, h)),
                  pl.BlockSpec((1, nkb, VT_ROWS, KEY_BLOCK), lambda b, h, qi: (b, 0, h, 0)),
                  pl.BlockSpec((HEAD_DIM, 1), lambda b, h, qi: (0, 0))],
        out_specs=pl.BlockSpec((tq, HEAD_DIM), lambda b, h, qi: (b * nq + qi, h)),
        scratch_shapes=[pltpu.VMEM((VT_ROWS, 2 * tq), F32),
                        pltpu.VMEM((1, 2 * tq), F32)],
        compiler_params=_cparams(("arbitrary", "arbitrary", "arbitrary")),
        name="diff_attn",
    )(jnp.stack([lam, k_bound]).astype(F32), q_all, k_all, vt_all,
      subln_g.reshape(HEAD_DIM, 1))


def _mixers_kernel(p_ref, pprev_ref, pnext_ref, u_ref, v_ref, pw_ref, ps_ref,
                   ng_ref, ws_ref, bst_ref, yp_ref, yg_ref, *, tiles_per_seq):
    tm = p_ref.shape[0]
    i = pl.program_id(0)
    nt, n_lat, nc = tiles_per_seq
    lat = i < n_lat
    pos = jnp.where(lat, i % nt, (i - n_lat) % nc)
    per = jnp.where(lat, nt, nc)
    first = pos == 0
    last = pos == per - 1

    r = lax.broadcasted_iota(jnp.int32, (tm, tm), 0)
    u = lax.broadcasted_iota(jnp.int32, (tm, tm), 1)
    rh = lax.broadcasted_iota(jnp.int32, (tm, 2 * POOL_HALO), 0)
    uh = lax.broadcasted_iota(jnp.int32, (tm, 2 * POOL_HALO), 1)
    uh_pos = jnp.where(uh < POOL_HALO, uh - POOL_HALO, tm + uh - POOL_HALO)
    uh_ok = jnp.where(uh < POOL_HALO, 1 - first.astype(jnp.int32), 1 - last.astype(jnp.int32)) > 0
    rcol = lax.broadcasted_iota(jnp.int32, (tm, 1), 0)
    seq_lo = jnp.where(first, 0, -POOL_HALO)
    seq_hi = jnp.where(last, tm - 1, tm - 1 + POOL_HALO)

    p = p_ref[...]
    halo = jnp.concatenate([pprev_ref[...], pnext_ref[...]], axis=0)

    def split(x):
        hi = x.astype(BF16)
        return hi, (x - hi.astype(F32)).astype(BF16)

    p_hi, p_lo = split(p)
    h_hi, h_lo = split(halo)
    for gi, win in enumerate(POOL_WINDOWS):
        sl = slice(gi * POOL_GROUP, (gi + 1) * POOL_GROUP)
        lo_off, hi_off = win // 2, win - 1 - win // 2
        band_c = jnp.where((u >= r - lo_off) & (u <= r + hi_off), 1.0, 0.0).astype(BF16)
        band_h = jnp.where((uh_pos >= rh - lo_off) & (uh_pos <= rh + hi_off) & uh_ok,
                           1.0, 0.0).astype(BF16)
        tot = (jnp.dot(band_c, p_hi[:, sl], preferred_element_type=F32)
               + jnp.dot(band_c, p_lo[:, sl], preferred_element_type=F32)
               + jnp.dot(band_h, h_hi[:, sl], preferred_element_type=F32)
               + jnp.dot(band_h, h_lo[:, sl], preferred_element_type=F32))
        lo = jnp.maximum(rcol - lo_off, seq_lo)
        hi = jnp.minimum(rcol + hi_off, seq_hi)
        cnt = (hi - lo + 1).astype(F32)
        dlt = tot / cnt - p[:, sl]
        y = jnp.dot(dlt.astype(BF16), pw_ref[gi], preferred_element_type=F32)
        yp_ref[:, sl] = (y * ps_ref[:, sl]).astype(BF16)

    uu = _gelu_tanh(u_ref[...])
    vv = _gelu_tanh(v_ref[...])
    mu = jnp.mean(vv, axis=-1, keepdims=True)
    vc = vv - mu
    vn = (vc * lax.rsqrt(jnp.mean(vc * vc, axis=-1, keepdims=True) + EPS) * ng_ref[...]).astype(BF16)
    for c in range(tm // GMLP_CHUNK):
        rows = slice(c * GMLP_CHUNK, (c + 1) * GMLP_CHUNK)
        for g in range(GMLP_GROUPS):
            sl = slice(g * GMLP_CHUNK, (g + 1) * GMLP_CHUNK)
            sv = jnp.dot(ws_ref[g], vn[rows, sl], preferred_element_type=F32) + bst_ref[:, g:g + 1]
            yg_ref[rows, sl] = (uu[rows, sl] * sv).astype(BF16)


def _mixers(proj, pool_w_bf16, pool_scale, gmlp_norm_g, gmlp_ws_bf16, gmlp_bs, *, t_rows,
            batch, n, lc):
    tm = ROW_TILE
    nt, nc = n // tm, lc // tm
    n_lat = batch * nt
    hb = tm // POOL_HALO
    n_hblocks = proj.shape[0] // POOL_HALO
    pool_cb, gu_cb, gv_cb = 3 * ATTN_WIDTH // POOL_WIDTH, 3 * ATTN_WIDTH // POOL_WIDTH + 1, \
        3 * ATTN_WIDTH // POOL_WIDTH + 2
    kernel = functools.partial(_mixers_kernel, tiles_per_seq=(nt, n_lat, nc))
    return pl.pallas_call(
        kernel,
        out_shape=(jax.ShapeDtypeStruct((t_rows, POOL_WIDTH), BF16),
                   jax.ShapeDtypeStruct((t_rows, GMLP_WIDTH), BF16)),
        grid=(t_rows // tm,),
        in_specs=[pl.BlockSpec((tm, POOL_WIDTH), lambda i: (i, pool_cb)),
                  pl.BlockSpec((POOL_HALO, POOL_WIDTH),
                               lambda i: (jnp.maximum(i * hb - 1, 0), pool_cb)),
                  pl.BlockSpec((POOL_HALO, POOL_WIDTH),
                               lambda i: (jnp.minimum((i + 1) * hb, n_hblocks - 1), pool_cb)),
                  pl.BlockSpec((tm, GMLP_WIDTH), lambda i: (i, gu_cb)),
                  pl.BlockSpec((tm, GMLP_WIDTH), lambda i: (i, gv_cb)),
                  pl.BlockSpec((len(POOL_WINDOWS), POOL_GROUP, POOL_GROUP), lambda i: (0, 0, 0)),
                  pl.BlockSpec((1, POOL_WIDTH), lambda i: (0, 0)),
                  pl.BlockSpec((1, GMLP_WIDTH), lambda i: (0, 0)),
                  pl.BlockSpec((GMLP_GROUPS, GMLP_CHUNK, GMLP_CHUNK), lambda i: (0, 0, 0)),
                  pl.BlockSpec((GMLP_CHUNK, GMLP_GROUPS), lambda i: (0, 0))],
        out_specs=(pl.BlockSpec((tm, POOL_WIDTH), lambda i: (i, 0)),
                   pl.BlockSpec((tm, GMLP_WIDTH), lambda i: (i, 0))),
        compiler_params=_cparams(("arbitrary",)),
        name="local_mixers",
    )(proj, proj, proj, proj, proj, pool_w_bf16, pool_scale.reshape(1, POOL_WIDTH),
      gmlp_norm_g.reshape(1, GMLP_WIDTH), gmlp_ws_bf16, gmlp_bs.T)


def _outproj_kernel(ya_ref, yp_ref, yg_ref, w_ref, x_ref, mod_ref, o_ref):
    acc = jnp.dot(ya_ref[...], w_ref[0:ATTN_WIDTH, :], preferred_element_type=F32)
    acc = acc + jnp.dot(yp_ref[...], w_ref[ATTN_WIDTH:ATTN_WIDTH + POOL_WIDTH, :],
                        preferred_element_type=F32)
    acc = acc + jnp.dot(yg_ref[...], w_ref[ATTN_WIDTH + POOL_WIDTH:, :],
                        preferred_element_type=F32)
    o_ref[...] = x_ref[...] + mod_ref[0, 2:3, :] * acc


def _outproj(y_attn, y_pool, y_gmlp, w_bf16, x, mod, seg_of_tile, *, t_rows):
    d = x.shape[1]
    tm = MM_ROW_TILE
    return pl.pallas_call(
        _outproj_kernel,
        out_shape=jax.ShapeDtypeStruct((t_rows, d), F32),
        grid=(t_rows // tm,),
        in_specs=[pl.BlockSpec((tm, ATTN_WIDTH), lambda i: (i, 0)),
                  pl.BlockSpec((tm, POOL_WIDTH), lambda i: (i, 0)),
                  pl.BlockSpec((tm, GMLP_WIDTH), lambda i: (i, 0)),
                  pl.BlockSpec(w_bf16.shape, lambda i: (0, 0)),
                  pl.BlockSpec((tm, d), lambda i: (i, 0)),
                  pl.BlockSpec((1, 6, d), lambda i: (seg_of_tile(i, tm), 0, 0))],
        out_specs=pl.BlockSpec((tm, d), lambda i: (i, 0)),
        compiler_params=_cparams(("arbitrary",)),
        name="out_proj",
    )(y_attn, y_pool, y_gmlp, w_bf16, x, mod)


def _router_kernel(x_ref, g_ref, mod_ref, wgh_ref, wgl_ref, weh_ref, wel_ref, bg_ref, be_ref,
                   h_ref, route_ref, cnt_ref, carry_ref):
    tm = x_ref.shape[0]

    @pl.when(pl.program_id(0) == 0)
    def _():
        carry_ref[...] = jnp.zeros(carry_ref.shape, F32)

    x = x_ref[...]
    ms = jnp.mean(x * x, axis=-1, keepdims=True)
    y = x * lax.rsqrt(ms + EPS) * g_ref[...]
    h = y * (1.0 + mod_ref[0, 4:5, :]) + mod_ref[0, 3:4, :]
    h_ref[...] = h

    hh = h.astype(BF16)
    hl = (h - hh.astype(F32)).astype(BF16)

    def logits(w_hi, w_lo, b):
        return (jnp.dot(hh, w_hi[...], preferred_element_type=F32)
                + jnp.dot(hl, w_hi[...], preferred_element_type=F32)
                + jnp.dot(hh, w_lo[...], preferred_element_type=F32) + b[...])

    gl = logits(wgh_ref, wgl_ref, bg_ref)
    el = logits(weh_ref, wel_ref, be_ref)
    lane = lax.broadcasted_iota(jnp.int32, (tm, LANES), 1).astype(F32)
    far = float(LANES)

    def first_argmax(v):
        vmax = jnp.max(v, axis=-1, keepdims=True)
        idx = jnp.min(jnp.where(v == vmax, lane, far), axis=-1, keepdims=True)
        return vmax, idx

    glm = jnp.where(lane < N_GROUPS, gl, NEG_BIG)
    gmax, g_idx = first_argmax(glm)
    g_val = 1.0 / jnp.sum(jnp.exp(glm - gmax), axis=-1, keepdims=True)
    in_grp = (jnp.floor(lane * (1.0 / EXPERTS_PER_GROUP)) == g_idx) & (lane < N_EXPERTS)
    elm = jnp.where(in_grp, el, NEG_BIG)
    m1, e1 = first_argmax(elm)
    elm2 = jnp.where(lane == e1, NEG_BIG, elm)
    m2, e2 = first_argmax(elm2)
    rr = jnp.exp(m2 - m1)
    w1 = g_val / (1.0 + rr)
    w2 = g_val * rr / (1.0 + rr)

    sel1 = lane == e1
    sel2 = lane == e2
    esum = jnp.where(sel1 | sel2, 1.0, 0.0)
    rI = lax.broadcasted_iota(jnp.int32, (tm, tm), 0)
    cI = lax.broadcasted_iota(jnp.int32, (tm, tm), 1)
    ltri = jnp.where(rI > cI, 1.0, 0.0).astype(BF16)
    before = jnp.dot(ltri, esum.astype(BF16), preferred_element_type=F32) + carry_ref[0:1, :]
    rank1 = jnp.sum(jnp.where(sel1, before, 0.0), axis=-1, keepdims=True)
    rank2 = jnp.sum(jnp.where(sel2, before, 0.0), axis=-1, keepdims=True)
    carry_ref[0:1, :] = carry_ref[0:1, :] + jnp.sum(esum, axis=0, keepdims=True)
    cnt_ref[...] = carry_ref[...]

    out = jnp.where(lane == 0, e1, 0.0)
    out = jnp.where(lane == 1, e2, out)
    out = jnp.where(lane == 2, rank1, out)
    out = jnp.where(lane == 3, rank2, out)
    out = jnp.where(lane == 4, w1, out)
    out = jnp.where(lane == 5, w2, out)
    route_ref[...] = out[:, 0:8]


def _router(x, norm_g, mod, rg_w, rg_b, re_w, re_b, seg_of_tile, *, t_rows):
    d = x.shape[1]
    tm = ROW_TILE

    def pad_split(w):
        wp = jnp.zeros((d, LANES), F32).at[:, :w.shape[1]].set(w)
        hi = wp.astype(BF16)
        return hi, (wp - hi.astype(F32)).astype(BF16)

    def pad_bias(b):
        return jnp.zeros((1, LANES), F32).at[0, :b.shape[0]].set(b)

    wgh, wgl = pad_split(rg_w)
    weh, wel = pad_split(re_w)
    wspec = pl.BlockSpec((d, LANES), lambda i: (0, 0))
    bspec = pl.BlockSpec((1, LANES), lambda i: (0, 0))
    return pl.pallas_call(
        _router_kernel,
        out_shape=(jax.ShapeDtypeStruct((t_rows, d), F32),
                   jax.ShapeDtypeStruct((t_rows, 8), F32),
                   jax.ShapeDtypeStruct((8, LANES), F32)),
        grid=(t_rows // tm,),
        in_specs=[pl.BlockSpec((tm, d), lambda i: (i, 0)),
                  pl.BlockSpec((1, d), lambda i: (0, 0)),
                  pl.BlockSpec((1, 6, d), lambda i: (seg_of_tile(i, tm), 0, 0)),
                  wspec, wspec, wspec, wspec, bspec, bspec],
        out_specs=(pl.BlockSpec((tm, d), lambda i: (i, 0)),
                   pl.BlockSpec((tm, 8), lambda i: (i, 0)),
                   pl.BlockSpec((8, LANES), lambda i: (0, 0))),
        scratch_shapes=[pltpu.VMEM((8, LANES), F32)],
        compiler_params=_cparams(("arbitrary",)),
        name="moe_router",
    )(x, norm_g.reshape(1, d), mod, wgh, wgl, weh, wel, pad_bias(rg_b), pad_bias(re_b))


def _dispatch_kernel(dest_ref, h_ref, xs_in_ref, xs_ref, sem):
    del xs_in_ref
    tm = h_ref.shape[0]
    base = pl.program_id(0) * (2 * tm)

    def row_copy(r, slot):
        return pltpu.make_async_copy(h_ref.at[pl.ds(r, 1)], xs_ref.at[pl.ds(slot, 1)], sem)

    def issue(r, carry):
        row_copy(r, dest_ref[base + 2 * r]).start()
        row_copy(r, dest_ref[base + 2 * r + 1]).start()
        return carry

    def drain(r, carry):
        row_copy(0, 0).wait()
        row_copy(0, 0).wait()
        return carry

    lax.fori_loop(0, tm, issue, 0)
    lax.fori_loop(0, tm, drain, 0)


def _dispatch(dest_flat, h2, n_slots):
    t, d = h2.shape
    tm = ROW_TILE
    return pl.pallas_call(
        _dispatch_kernel,
        out_shape=jax.ShapeDtypeStruct((n_slots, d), F32),
        grid_spec=pltpu.PrefetchScalarGridSpec(
            num_scalar_prefetch=1,
            grid=(t // tm,),
            in_specs=[pl.BlockSpec((tm, d), lambda i, dest: (i, 0)),
                      pl.BlockSpec(memory_space=pl.ANY)],
            out_specs=pl.BlockSpec(memory_space=pl.ANY),
            scratch_shapes=[pltpu.SemaphoreType.DMA]),
        input_output_aliases={2: 0},
        compiler_params=_cparams(("arbitrary",)),
        name="moe_dispatch",
    )(dest_flat, h2, jnp.zeros((n_slots, d), F32))


def _expert_kernel(blk_e_ref, n_used_ref, xs_ref, wg_ref, wu_ref, wd_ref, ys_ref,
                   wg_bf, wu_bf, wd_bf):
    i = pl.program_id(0)
    used = i < n_used_ref[0]
    prev_e = blk_e_ref[jnp.maximum(i - 1, 0)]
    new_expert = jnp.logical_or(i == 0, blk_e_ref[i] != prev_e)

    @pl.when(jnp.logical_and(used, new_expert))
    def _():
        wg_bf[...] = wg_ref[0, 0].astype(BF16)
        wu_bf[...] = wu_ref[0, 0].astype(BF16)
        wd_bf[...] = wd_ref[0, 0].astype(BF16)

    @pl.when(used)
    def _():
        xb = xs_ref[...].astype(BF16)
        a = jnp.dot(xb, wg_bf[...], preferred_element_type=F32)
        b = jnp.dot(xb, wu_bf[...], preferred_element_type=F32)
        hmid = (a * jax.nn.sigmoid(a) * b).astype(BF16)
        ys_ref[...] = jnp.dot(hmid, wd_bf[...], preferred_element_type=F32)

    @pl.when(jnp.logical_not(used))
    def _():
        ys_ref[...] = jnp.zeros(ys_ref.shape, F32)


def _experts(blk_e, n_used, xs, w_gate, w_up, w_down, layer):
    n_slots, d = xs.shape
    ff = w_gate.shape[3]
    return pl.pallas_call(
        _expert_kernel,
        out_shape=jax.ShapeDtypeStruct((n_slots, d), F32),
        grid_spec=pltpu.PrefetchScalarGridSpec(
            num_scalar_prefetch=2,
            grid=(n_slots // MOE_BLOCK,),
            in_specs=[pl.BlockSpec((MOE_BLOCK, d), lambda i, be, nu: (i, 0)),
                      pl.BlockSpec((1, 1, d, ff), lambda i, be, nu: (layer, be[i], 0, 0)),
                      pl.BlockSpec((1, 1, d, ff), lambda i, be, nu: (layer, be[i], 0, 0)),
                      pl.BlockSpec((1, 1, ff, d), lambda i, be, nu: (layer, be[i], 0, 0))],
            out_specs=pl.BlockSpec((MOE_BLOCK, d), lambda i, be, nu: (i, 0)),
            scratch_shapes=[pltpu.VMEM((d, ff), BF16), pltpu.VMEM((d, ff), BF16),
                            pltpu.VMEM((ff, d), BF16)]),
        compiler_params=_cparams(("arbitrary",)),
        name="moe_experts",
    )(blk_e, n_used, xs, w_gate, w_up, w_down)


def _combine_kernel(dest_ref, x_ref, mod_ref, route_ref, ys_ref, o_ref, y0_ref, y1_ref, sem):
    tm = x_ref.shape[0]
    base = pl.program_id(0) * (2 * tm)

    def row_copy(slot, buf, r):
        return pltpu.make_async_copy(ys_ref.at[pl.ds(slot, 1)], buf.at[pl.ds(r, 1)], sem)

    def issue(r, carry):
        row_copy(dest_ref[base + 2 * r], y0_ref, r).start()
        row_copy(dest_ref[base + 2 * r + 1], y1_ref, r).start()
        return carry

    def drain(r, carry):
        row_copy(0, y0_ref, 0).wait()
        row_copy(0, y1_ref, 0).wait()
        return carry

    lax.fori_loop(0, tm, issue, 0)
    lax.fori_loop(0, tm, drain, 0)
    f = route_ref[:, 4:5] * y0_ref[...] + route_ref[:, 5:6] * y1_ref[...]
    o_ref[...] = x_ref[...] + mod_ref[0, 5:6, :] * f


def _combine(dest_flat, x, mod, route, ys, seg_of_tile):
    t, d = x.shape
    tm = ROW_TILE
    return pl.pallas_call(
        _combine_kernel,
        out_shape=jax.ShapeDtypeStruct((t, d), F32),
        grid_spec=pltpu.PrefetchScalarGridSpec(
            num_scalar_prefetch=1,
            grid=(t // tm,),
            in_specs=[pl.BlockSpec((tm, d), lambda i, dest: (i, 0)),
                      pl.BlockSpec((1, 6, d), lambda i, dest: (seg_of_tile(i, tm), 0, 0)),
                      pl.BlockSpec((tm, 8), lambda i, dest: (i, 0)),
                      pl.BlockSpec(memory_space=pl.ANY)],
            out_specs=pl.BlockSpec((tm, d), lambda i, dest: (i, 0)),
            scratch_shapes=[pltpu.VMEM((tm, d), F32), pltpu.VMEM((tm, d), F32),
                            pltpu.SemaphoreType.DMA]),
        compiler_params=_cparams(("arbitrary",)),
        name="moe_combine",
    )(dest_flat, x, mod, route, ys)


def _moe(x, norm_g, mod, rg_w, rg_b, re_w, re_b, w_gate, w_up, w_down, layer, seg_of_tile):
    t = x.shape[0]
    h2, route, cnt = _router(x, norm_g, mod, rg_w, rg_b, re_w, re_b, seg_of_tile, t_rows=t)
    counts = cnt[0, :N_EXPERTS].astype(jnp.int32)
    padded = (counts + MOE_BLOCK - 1) // MOE_BLOCK * MOE_BLOCK
    pend = jnp.cumsum(padded)
    pstart = pend - padded
    n_slots = -(-(2 * t + N_EXPERTS * (MOE_BLOCK - 1)) // MOE_BLOCK) * MOE_BLOCK
    n_blk = n_slots // MOE_BLOCK
    eid = route[:, 0:2].astype(jnp.int32)
    rank = route[:, 2:4].astype(jnp.int32)
    dest = (pstart[eid] + rank).reshape(-1)
    blk_start = jnp.arange(n_blk, dtype=jnp.int32) * MOE_BLOCK
    blk_e = jnp.minimum(jnp.sum((pend[None, :] <= blk_start[:, None]).astype(jnp.int32), axis=1),
                        N_EXPERTS - 1)
    n_used = (pend[-1] // MOE_BLOCK).astype(jnp.int32).reshape(1)
    xs = _dispatch(dest, h2, n_slots)
    ys = _experts(blk_e, n_used, xs, w_gate, w_up, w_down, layer)
    return _combine(dest, x, mod, route, ys, seg_of_tile)


def kernel(x, c, ctx, c_ctx, ada_w, ada_b, norm1_g, norm2_g, w_in, q_norm_g, k_norm_g, lam_q1, lam_k1, lam_q2, lam_k2, subln_g, pool_w, pool_scale, gmlp_norm_g, gmlp_ws, gmlp_bs, w_out, router_g_w, router_g_b, router_e_w, router_e_b, w_gate, w_up, w_down):
    batch, n, d = x.shape
    lc = ctx.shape[1]
    depth = ada_w.shape[0]
    t_lat = batch * n
    t_all = t_lat + batch * lc
    assert n % MM_ROW_TILE == 0 and (batch * lc) % MM_ROW_TILE == 0 and lc % KEY_BLOCK == 0
    assert n % Q_TILE == 0 and lc % Q_TILE == 0 and n % GRID_W == 0

    def seg_of_tile(i, tm):
        return jnp.minimum(i // (n // tm), batch)

    xs = jnp.concatenate([x.reshape(t_lat, d), ctx.reshape(batch * lc, d)], axis=0)
    c_all = jnp.concatenate([c, c_ctx[None, :]], axis=0)
    cos_t, sin_t = _rope_tables(n, KEY_BLOCK)

    for l in range(depth):
        last = l == depth - 1
        lam_init = 0.8 - 0.6 * math.exp(-0.3 * l)
        lam = (jnp.exp(jnp.sum(lam_q1[l] * lam_k1[l])) - jnp.exp(jnp.sum(lam_q2[l] * lam_k2[l]))
               + lam_init).astype(F32)
        k_bound = jnp.max(jnp.abs(k_norm_g[l])) * (math.sqrt(QK_DIM) * 1.01)
        mod = _ada(c_all, ada_w, ada_b[l], l)
        proj = _inproj(xs, norm1_g[l], mod, w_in[l].astype(BF16), seg_of_tile)
        q_all, k_all, vt_all = _prep(proj, cos_t, sin_t,
                                     jnp.tile(q_norm_g[l], 2).reshape(1, LANES),
                                     jnp.tile(k_norm_g[l], 2).reshape(1, LANES),
                                     batch=batch, n=n, lc=lc)
        t_rows = t_lat if last else t_all
        y_attn = _attention(lam, k_bound, q_all, k_all, vt_all, subln_g[l], batch=batch,
                            q_row0=0, lq=n, lk=lc + n, lam_init=lam_init)
        if not last:
            y_ctx = _attention(lam, k_bound, q_all, k_all, vt_all, subln_g[l], batch=batch,
                               q_row0=t_lat, lq=lc, lk=lc, lam_init=lam_init)
            y_attn = jnp.concatenate([y_attn, y_ctx], axis=0)
        y_pool, y_gmlp = _mixers(proj, pool_w[l].astype(BF16), pool_scale[l], gmlp_norm_g[l],
                                 gmlp_ws[l].astype(BF16), gmlp_bs[l], t_rows=t_rows,
                                 batch=batch, n=n, lc=lc)
        x1 = _outproj(y_attn, y_pool, y_gmlp, w_out[l].astype(BF16), xs, mod, seg_of_tile,
                      t_rows=t_rows)
        xs = _moe(x1, norm2_g[l], mod, router_g_w[l], router_g_b[l], router_e_w[l],
                  router_e_b[l], w_gate, w_up, w_down, l, seg_of_tile)
    return xs[:t_lat].reshape(batch, n, d)
```

```python
import functools
import math

import jax
import jax.numpy as jnp
from jax import lax
from jax.experimental import pallas as pl
from jax.experimental.pallas import tpu as pltpu

F32 = jnp.float32
BF16 = jnp.bfloat16

EPS = 1e-6
GRID_W = 64
ROPE_BASE = 10000.0
N_HEADS = 8
HEAD_DIM = 128
QK_DIM = 64
ROPE_HALF = 16
ATTN_WIDTH = N_HEADS * HEAD_DIM
POOL_WINDOWS = (2, 4, 8, 16)
POOL_GROUP = 128
POOL_WIDTH = 512
POOL_HALO = 8
GMLP_WIDTH = 512
GMLP_GROUPS = 4
GMLP_CHUNK = 128
N_GROUPS = 4
EXPERTS_PER_GROUP = 8
N_EXPERTS = 32
MOE_BLOCK = 256
KEY_BLOCK = 256
FAST_SKEW = 2
FAST_GROUP_MAX = 65
FAST_MIN_DENOM = 2.0 ** -40
FAST_SHIFT_FRACTION = 0.2
FAST_MAX_EXPONENT = 100.0
MAX_KB_PER_TILE = 5
VT_ROWS = HEAD_DIM + 16
LANES = 128
NEG_BIG = -1e30
Q_SCALE_LOG2 = QK_DIM ** -0.5 * math.log2(math.e)

VMEM_LIMIT_V7X = 56 * 1024 * 1024

ROW_TILE = 256
MM_ROW_TILE = 512
Q_TILE = 256


def _cparams(semantics, vmem=VMEM_LIMIT_V7X, flags=None):
    return pltpu.CompilerParams(dimension_semantics=semantics, vmem_limit_bytes=vmem, flags=flags)


def _gelu_tanh(x):
    return 0.5 * x * (1.0 + jnp.tanh(math.sqrt(2.0 / math.pi) * (x + 0.044715 * x * x * x)))


def _ada_kernel(ct_ref, w_ref, b_ref, o_ref, *, nseg):
    ct = ct_ref[...]
    at = ct * jax.nn.sigmoid(ct)
    d = w_ref.shape[0]
    chunk = 256
    accs = [jnp.zeros((1, w_ref.shape[1]), F32) for _ in range(nseg)]
    for kc in range(d // chunk):
        w = w_ref[kc * chunk:(kc + 1) * chunk, :]
        for r in range(nseg):
            accs[r] = accs[r] + jnp.sum(at[kc * chunk:(kc + 1) * chunk, r:r + 1] * w,
                                        axis=0, keepdims=True)
    o_ref[...] = jnp.zeros(o_ref.shape, F32)
    for r in range(nseg):
        o_ref[r:r + 1, :] = accs[r] + b_ref[...]


def _ada(c_all, ada_w, ada_b, layer):
    nseg, d = c_all.shape
    n_out = ada_w.shape[2]
    tn = 768
    ct = jnp.zeros((d, 8), F32).at[:, :nseg].set(c_all.T)
    out = pl.pallas_call(
        functools.partial(_ada_kernel, nseg=nseg),
        out_shape=jax.ShapeDtypeStruct((8, n_out), F32),
        grid=(n_out // tn,),
        in_specs=[pl.BlockSpec((d, 8), lambda j: (0, 0)),
                  pl.BlockSpec((None, d, tn), lambda j: (layer, 0, j)),
                  pl.BlockSpec((1, tn), lambda j: (0, j))],
        out_specs=pl.BlockSpec((8, tn), lambda j: (0, j)),
        compiler_params=_cparams(("arbitrary",)),
        name="ada_mod",
    )(ct, ada_w, ada_b.reshape(1, n_out))
    return out[:nseg].reshape(nseg, 6, d)


def _inproj_kernel(x_ref, g_ref, mod_ref, w_ref, o_ref, h_ref):
    @pl.when(pl.program_id(1) == 0)
    def _():
        x = x_ref[...]
        ms = jnp.mean(x * x, axis=-1, keepdims=True)
        y = x * lax.rsqrt(ms + EPS) * g_ref[...]
        h_ref[...] = (y * (1.0 + mod_ref[0, 1:2, :]) + mod_ref[0, 0:1, :]).astype(BF16)

    o_ref[...] = jnp.dot(h_ref[...], w_ref[...], preferred_element_type=F32)


def _inproj(x, norm_g, mod, w_bf16, seg_of_tile):
    t, d = x.shape
    n_out = w_bf16.shape[1]
    tm, tn = MM_ROW_TILE, 2304
    return pl.pallas_call(
        _inproj_kernel,
        out_shape=jax.ShapeDtypeStruct((t, n_out), F32),
        grid=(t // tm, n_out // tn),
        in_specs=[pl.BlockSpec((tm, d), lambda i, j: (i, 0)),
                  pl.BlockSpec((1, d), lambda i, j: (0, 0)),
                  pl.BlockSpec((1, 6, d), lambda i, j: (seg_of_tile(i, tm), 0, 0)),
                  pl.BlockSpec((d, tn), lambda i, j: (0, j))],
        out_specs=pl.BlockSpec((tm, tn), lambda i, j: (i, j)),
        scratch_shapes=[pltpu.VMEM((tm, d), BF16)],
        compiler_params=_cparams(("arbitrary", "arbitrary")),
        name="in_proj",
    )(x, norm_g.reshape(1, d), mod, w_bf16)


def _prep_kernel(q_ref, k_ref, v_ref, cos_ref, sin_ref, qg_ref, kg_ref,
                 qo_ref, ko_ref, vo_ref):
    tm = q_ref.shape[0]
    cos = cos_ref[...]
    sin = sin_ref[...]
    lane = lax.broadcasted_iota(jnp.int32, (tm, LANES), 1)
    first_half = (lane % (2 * ROPE_HALF)) < ROPE_HALF
    r_i = lax.broadcasted_iota(jnp.int32, (LANES, LANES), 0) // QK_DIM
    c_i = lax.broadcasted_iota(jnp.int32, (LANES, LANES), 1) // QK_DIM
    group_ones = jnp.where(r_i == c_i, 1.0, 0.0).astype(BF16)

    def norm_rope(x, g, scale):
        sq = x * x
        hi = sq.astype(BF16)
        lo = (sq - hi.astype(F32)).astype(BF16)
        ss = (jnp.dot(hi, group_ones, preferred_element_type=F32)
              + jnp.dot(lo, group_ones, preferred_element_type=F32))
        y = x * lax.rsqrt(ss * (1.0 / QK_DIM) + EPS) * g
        partner = jnp.where(first_half,
                            pltpu.roll(y, LANES - ROPE_HALF, 1),
                            pltpu.roll(y, ROPE_HALF, 1))
        return (y * cos + partner * sin) * scale

    qg = qg_ref[...]
    kg = kg_ref[...]
    pad_row = lax.broadcasted_iota(jnp.int32, (VT_ROWS - HEAD_DIM, tm), 0)
    ones_rows = jnp.where(pad_row == 0, 1.0, 0.0).astype(BF16)
    for h in range(N_HEADS):
        sl = slice(h * HEAD_DIM, (h + 1) * HEAD_DIM)
        qo_ref[:, sl] = norm_rope(q_ref[:, sl], qg, Q_SCALE_LOG2).astype(BF16)
        ko_ref[0, :, sl] = norm_rope(k_ref[:, sl], kg, 1.0).astype(BF16)
        vo_ref[0, 0, h * VT_ROWS:h * VT_ROWS + HEAD_DIM, :] = v_ref[:, sl].T.astype(BF16)
        vo_ref[0, 0, h * VT_ROWS + HEAD_DIM:(h + 1) * VT_ROWS, :] = ones_rows


def _prep(proj, cos_t, sin_t, qg, kg, *, batch, n, lc):
    t = proj.shape[0]
    tm = KEY_BLOCK
    nt, nc = n // tm, lc // tm
    n_lat_tiles = batch * nt
    lk = lc + n
    nkb = lk // KEY_BLOCK

    def kv_block(i):
        lat = i < n_lat_tiles
        ic = i - n_lat_tiles
        b = jnp.where(lat, i // nt, ic // nc)
        kb = jnp.where(lat, nc + i % nt, ic % nc)
        return b, kb

    def tab_block(i):
        return jnp.where(i < n_lat_tiles, i % nt, nt)

    return pl.pallas_call(
        _prep_kernel,
        out_shape=(jax.ShapeDtypeStruct((t, ATTN_WIDTH), BF16),
                   jax.ShapeDtypeStruct((batch, lk, ATTN_WIDTH), BF16),
                   jax.ShapeDtypeStruct((batch, nkb, N_HEADS * VT_ROWS, KEY_BLOCK), BF16)),
        grid=(t // tm,),
        in_specs=[pl.BlockSpec((tm, ATTN_WIDTH), lambda i: (i, 0)),
                  pl.BlockSpec((tm, ATTN_WIDTH), lambda i: (i, 1)),
                  pl.BlockSpec((tm, ATTN_WIDTH), lambda i: (i, 2)),
                  pl.BlockSpec((tm, LANES), lambda i: (tab_block(i), 0)),
                  pl.BlockSpec((tm, LANES), lambda i: (tab_block(i), 0)),
                  pl.BlockSpec((1, LANES), lambda i: (0, 0)),
                  pl.BlockSpec((1, LANES), lambda i: (0, 0))],
        out_specs=(pl.BlockSpec((tm, ATTN_WIDTH), lambda i: (i, 0)),
                   pl.BlockSpec((1, tm, ATTN_WIDTH), lambda i: kv_block(i) + (0,)),
                   pl.BlockSpec((1, 1, N_HEADS * VT_ROWS, KEY_BLOCK),
                                lambda i: kv_block(i) + (0, 0))),
        compiler_params=_cparams(("arbitrary",)),
        name="qkv_prep",
    )(proj, proj, proj, cos_t, sin_t, qg, kg)


def _rope_tables(n, tm):
    pos = jnp.arange(n, dtype=jnp.int32)
    row = (pos // GRID_W).astype(F32)
    col = (pos % GRID_W).astype(F32)
    inv = ROPE_BASE ** (-jnp.arange(ROPE_HALF, dtype=F32) / ROPE_HALF)
    ang_r = row[:, None] * inv[None, :]
    ang_c = col[:, None] * inv[None, :]
    ang = jnp.concatenate([ang_r, ang_r, ang_c, ang_c], axis=-1)
    sign = jnp.tile(jnp.concatenate([-jnp.ones(ROPE_HALF, F32), jnp.ones(ROPE_HALF, F32)]), 2)
    cos = jnp.tile(jnp.cos(ang), (1, 2))
    sin = jnp.tile(jnp.sin(ang) * sign[None, :], (1, 2))
    cos = jnp.concatenate([cos, jnp.ones((tm, LANES), F32)], axis=0)
    sin = jnp.concatenate([sin, jnp.zeros((tm, LANES), F32)], axis=0)
    return cos, sin


NT_DIMS = (((1,), (1,)), ((), ()))


def _attn_online_pass(k_ref, vt_ref, qbd, acc_ref, m_ref, *, n_tiles, kb_per_tile):
    tk = kb_per_tile * KEY_BLOCK
    m_ref[...] = jnp.full(m_ref.shape, NEG_BIG, F32)
    acc_ref[...] = jnp.zeros(acc_ref.shape, F32)

    def tile(t, carry):
        kt = k_ref[0, pl.ds(pl.multiple_of(t * tk, tk), tk), :]
        s = lax.dot_general(kt, qbd, NT_DIMS, preferred_element_type=F32)
        m_old = m_ref[...]
        m_new = jnp.maximum(m_old, jnp.max(s, axis=0, keepdims=True))
        p = jnp.exp2((s - m_new).astype(BF16))
        vt = jnp.concatenate([vt_ref[0, t * kb_per_tile + c] for c in range(kb_per_tile)], axis=1)
        acc_ref[...] = (jnp.exp2(m_old - m_new) * acc_ref[...]
                        + jnp.dot(vt, p, preferred_element_type=F32))
        m_ref[...] = m_new
        return carry

    lax.fori_loop(0, n_tiles, tile, 0)


def _attn_kernel(par_ref, q_ref, k_ref, vt_ref, g_ref, o_ref, acc_ref, m_ref,
                 *, n_blocks, kb_per_tile, out_scale):
    tq = q_ref.shape[0]
    q = q_ref[...]
    lane = lax.broadcasted_iota(jnp.int32, (tq, HEAD_DIM), 1)
    zero = jnp.zeros_like(q)
    qbd = jnp.concatenate([jnp.where(lane < QK_DIM, q, zero),
                           jnp.where(lane >= QK_DIM, q, zero)], axis=0)

    qf = q.astype(F32)
    qsq = qf * qf
    qsq_hi = qsq.astype(BF16)
    qsq_lo = (qsq - qsq_hi.astype(F32)).astype(BF16)
    sel_r = lax.broadcasted_iota(jnp.int32, (8, HEAD_DIM), 0)
    sel_l = lax.broadcasted_iota(jnp.int32, (8, HEAD_DIM), 1)
    sel = jnp.where(sel_l // QK_DIM == sel_r, 1.0, 0.0).astype(BF16)
    qn2 = (lax.dot_general(sel, qsq_hi, NT_DIMS, preferred_element_type=F32)
           + lax.dot_general(sel, qsq_lo, NT_DIMS, preferred_element_type=F32))
    k_bound = par_ref[1]
    bound = jnp.concatenate([jnp.sqrt(qn2[0:1]) * k_bound, jnp.sqrt(qn2[1:2]) * k_bound],
                            axis=1)
    shift = jnp.maximum(FAST_SHIFT_FRACTION * bound, bound - FAST_MAX_EXPONENT)
    acc_ref[...] = jnp.zeros(acc_ref.shape, F32)
    group = max(c for c in range(1, FAST_GROUP_MAX + 1) if n_blocks % c == 0)

    def fast_group(u, carry):
        pv = None
        scores = {}
        for cc in range(group + FAST_SKEW):
            if cc < group:
                row0 = pl.multiple_of((u * group + cc) * KEY_BLOCK, KEY_BLOCK)
                kt = k_ref[0, pl.ds(row0, KEY_BLOCK), :]
                scores[cc] = lax.dot_general(kt, qbd, NT_DIMS, preferred_element_type=F32)
            c = cc - FAST_SKEW
            if c >= 0:
                p = jnp.exp2((scores.pop(c) - shift).astype(BF16))
                d = jnp.dot(vt_ref[0, u * group + c], p, preferred_element_type=F32)
                pv = d if pv is None else pv + d
        acc_ref[...] += pv
        return carry

    if n_blocks == group:
        fast_group(0, 0)
    else:
        lax.fori_loop(0, n_blocks // group, fast_group, 0)
    denom_min = jnp.min(acc_ref[HEAD_DIM:HEAD_DIM + 1, :])

    @pl.when(jnp.logical_not(denom_min >= FAST_MIN_DENOM))
    def _():
        _attn_online_pass(k_ref, vt_ref, qbd, acc_ref, m_ref,
                          n_tiles=n_blocks // kb_per_tile, kb_per_tile=kb_per_tile)

    o = acc_ref[0:HEAD_DIM, :] / acc_ref[HEAD_DIM:HEAD_DIM + 1, :]
    d = o[:, :tq] - par_ref[0] * o[:, tq:]
    ms = jnp.mean(d * d, axis=0, keepdims=True)
    y = d * lax.rsqrt(ms + EPS) * (g_ref[...] * out_scale)
    o_ref[...] = y.T.astype(BF16)


def _attention(lam, k_bound, q_all, k_all, vt_all, subln_g, *, batch, q_row0, lq, lk, lam_init):
    tq = Q_TILE
    nq = lq // tq
    nkb = lk // KEY_BLOCK
    kb_per_tile = max(c for c in range(1, MAX_KB_PER_TILE + 1) if nkb % c == 0)
    qb0 = q_row0 // tq
    kernel = functools.partial(_attn_kernel, n_blocks=nkb, kb_per_tile=kb_per_tile,
                               out_scale=1.0 - lam_init)
    return pl.pallas_call(
        kernel,
        out_shape=jax.ShapeDtypeStruct((batch * lq, ATTN_WIDTH), BF16),
        grid=(batch, N_HEADS, nq),
        in_specs=[pl.BlockSpec(memory_space=pltpu.SMEM),
                  pl.BlockSpec((tq, HEAD_DIM), lambda b, h, qi: (qb0 + b * nq + qi, h)),
                  pl.BlockSpec((1, lk, HEAD_DIM), lambda b, h, qi: (b, 0, h)),
                  pl.BlockSpec((1, nkb, VT_ROWS, KEY_BLOCK), lambda b, h, qi: (b, 0, h, 0)),
                  pl.BlockSpec((HEAD_DIM, 1), lambda b, h, qi: (0, 0))],
        out_specs=pl.BlockSpec((tq, HEAD_DIM), lambda b, h, qi: (b * nq + qi, h)),
        scratch_shapes=[pltpu.VMEM((VT_ROWS, 2 * tq), F32),
                        pltpu.VMEM((1, 2 * tq), F32)],
        compiler_params=_cparams(("arbitrary", "arbitrary", "arbitrary")),
        name="diff_attn",
    )(jnp.stack([lam, k_bound]).astype(F32), q_all, k_all, vt_all,
      subln_g.reshape(HEAD_DIM, 1))


def _mixers_kernel(p_ref, pprev_ref, pnext_ref, u_ref, v_ref, pw_ref, ps_ref,
                   ng_ref, ws_ref, bst_ref, yp_ref, yg_ref, *, tiles_per_seq):
    tm = p_ref.shape[0]
    i = pl.program_id(0)
    nt, n_lat, nc = tiles_per_seq
    lat = i < n_lat
    pos = jnp.where(lat, i % nt, (i - n_lat) % nc)
    per = jnp.where(lat, nt, nc)
    first = pos == 0
    last = pos == per - 1

    r = lax.broadcasted_iota(jnp.int32, (tm, tm), 0)
    u = lax.broadcasted_iota(jnp.int32, (tm, tm), 1)
    rh = lax.broadcasted_iota(jnp.int32, (tm, 2 * POOL_HALO), 0)
    uh = lax.broadcasted_iota(jnp.int32, (tm, 2 * POOL_HALO), 1)
    uh_pos = jnp.where(uh < POOL_HALO, uh - POOL_HALO, tm + uh - POOL_HALO)
    uh_ok = jnp.where(uh < POOL_HALO, 1 - first.astype(jnp.int32), 1 - last.astype(jnp.int32)) > 0
    rcol = lax.broadcasted_iota(jnp.int32, (tm, 1), 0)
    seq_lo = jnp.where(first, 0, -POOL_HALO)
    seq_hi = jnp.where(last, tm - 1, tm - 1 + POOL_HALO)

    p = p_ref[...]
    halo = jnp.concatenate([pprev_ref[...], pnext_ref[...]], axis=0)

    def split(x):
        hi = x.astype(BF16)
        return hi, (x - hi.astype(F32)).astype(BF16)

    p_hi, p_lo = split(p)
    h_hi, h_lo = split(halo)
    for gi, win in enumerate(POOL_WINDOWS):
        sl = slice(gi * POOL_GROUP, (gi + 1) * POOL_GROUP)
        lo_off, hi_off = win // 2, win - 1 - win // 2
        band_c = jnp.where((u >= r - lo_off) & (u <= r + hi_off), 1.0, 0.0).astype(BF16)
        band_h = jnp.where((uh_pos >= rh - lo_off) & (uh_pos <= rh + hi_off) & uh_ok,
                           1.0, 0.0).astype(BF16)
        tot = (jnp.dot(band_c, p_hi[:, sl], preferred_element_type=F32)
               + jnp.dot(band_c, p_lo[:, sl], preferred_element_type=F32)
               + jnp.dot(band_h, h_hi[:, sl], preferred_element_type=F32)
               + jnp.dot(band_h, h_lo[:, sl], preferred_element_type=F32))
        lo = jnp.maximum(rcol - lo_off, seq_lo)
        hi = jnp.minimum(rcol + hi_off, seq_hi)
        cnt = (hi - lo + 1).astype(F32)
        dlt = tot / cnt - p[:, sl]
        y = jnp.dot(dlt.astype(BF16), pw_ref[gi], preferred_element_type=F32)
        yp_ref[:, sl] = (y * ps_ref[:, sl]).astype(BF16)

    uu = _gelu_tanh(u_ref[...])
    vv = _gelu_tanh(v_ref[...])
    mu = jnp.mean(vv, axis=-1, keepdims=True)
    vc = vv - mu
    vn = (vc * lax.rsqrt(jnp.mean(vc * vc, axis=-1, keepdims=True) + EPS) * ng_ref[...]).astype(BF16)
    for c in range(tm // GMLP_CHUNK):
        rows = slice(c * GMLP_CHUNK, (c + 1) * GMLP_CHUNK)
        for g in range(GMLP_GROUPS):
            sl = slice(g * GMLP_CHUNK, (g + 1) * GMLP_CHUNK)
            sv = jnp.dot(ws_ref[g], vn[rows, sl], preferred_element_type=F32) + bst_ref[:, g:g + 1]
            yg_ref[rows, sl] = (uu[rows, sl] * sv).astype(BF16)


def _mixers(proj, pool_w_bf16, pool_scale, gmlp_norm_g, gmlp_ws_bf16, gmlp_bs, *, t_rows,
            batch, n, lc):
    tm = ROW_TILE
    nt, nc = n // tm, lc // tm
    n_lat = batch * nt
    hb = tm // POOL_HALO
    n_hblocks = proj.shape[0] // POOL_HALO
    pool_cb, gu_cb, gv_cb = 3 * ATTN_WIDTH // POOL_WIDTH, 3 * ATTN_WIDTH // POOL_WIDTH + 1, \
        3 * ATTN_WIDTH // POOL_WIDTH + 2
    kernel = functools.partial(_mixers_kernel, tiles_per_seq=(nt, n_lat, nc))
    return pl.pallas_call(
        kernel,
        out_shape=(jax.ShapeDtypeStruct((t_rows, POOL_WIDTH), BF16),
                   jax.ShapeDtypeStruct((t_rows, GMLP_WIDTH), BF16)),
        grid=(t_rows // tm,),
        in_specs=[pl.BlockSpec((tm, POOL_WIDTH), lambda i: (i, pool_cb)),
                  pl.BlockSpec((POOL_HALO, POOL_WIDTH),
                               lambda i: (jnp.maximum(i * hb - 1, 0), pool_cb)),
                  pl.BlockSpec((POOL_HALO, POOL_WIDTH),
                               lambda i: (jnp.minimum((i + 1) * hb, n_hblocks - 1), pool_cb)),
                  pl.BlockSpec((tm, GMLP_WIDTH), lambda i: (i, gu_cb)),
                  pl.BlockSpec((tm, GMLP_WIDTH), lambda i: (i, gv_cb)),
                  pl.BlockSpec((len(POOL_WINDOWS), POOL_GROUP, POOL_GROUP), lambda i: (0, 0, 0)),
                  pl.BlockSpec((1, POOL_WIDTH), lambda i: (0, 0)),
                  pl.BlockSpec((1, GMLP_WIDTH), lambda i: (0, 0)),
                  pl.BlockSpec((GMLP_GROUPS, GMLP_CHUNK, GMLP_CHUNK), lambda i: (0, 0, 0)),
                  pl.BlockSpec((GMLP_CHUNK, GMLP_GROUPS), lambda i: (0, 0))],
        out_specs=(pl.BlockSpec((tm, POOL_WIDTH), lambda i: (i, 0)),
                   pl.BlockSpec((tm, GMLP_WIDTH), lambda i: (i, 0))),
        compiler_params=_cparams(("arbitrary",)),
        name="local_mixers",
    )(proj, proj, proj, proj, proj, pool_w_bf16, pool_scale.reshape(1, POOL_WIDTH),
      gmlp_norm_g.reshape(1, GMLP_WIDTH), gmlp_ws_bf16, gmlp_bs.T)


def _outproj_kernel(ya_ref, yp_ref, yg_ref, w_ref, x_ref, mod_ref, o_ref):
    acc = jnp.dot(ya_ref[...], w_ref[0:ATTN_WIDTH, :], preferred_element_type=F32)
    acc = acc + jnp.dot(yp_ref[...], w_ref[ATTN_WIDTH:ATTN_WIDTH + POOL_WIDTH, :],
                        preferred_element_type=F32)
    acc = acc + jnp.dot(yg_ref[...], w_ref[ATTN_WIDTH + POOL_WIDTH:, :],
                        preferred_element_type=F32)
    o_ref[...] = x_ref[...] + mod_ref[0, 2:3, :] * acc


def _outproj(y_attn, y_pool, y_gmlp, w_bf16, x, mod, seg_of_tile, *, t_rows):
    d = x.shape[1]
    tm = MM_ROW_TILE
    return pl.pallas_call(
        _outproj_kernel,
        out_shape=jax.ShapeDtypeStruct((t_rows, d), F32),
        grid=(t_rows // tm,),
        in_specs=[pl.BlockSpec((tm, ATTN_WIDTH), lambda i: (i, 0)),
                  pl.BlockSpec((tm, POOL_WIDTH), lambda i: (i, 0)),
                  pl.BlockSpec((tm, GMLP_WIDTH), lambda i: (i, 0)),
                  pl.BlockSpec(w_bf16.shape, lambda i: (0, 0)),
                  pl.BlockSpec((tm, d), lambda i: (i, 0)),
                  pl.BlockSpec((1, 6, d), lambda i: (seg_of_tile(i, tm), 0, 0))],
        out_specs=pl.BlockSpec((tm, d), lambda i: (i, 0)),
        compiler_params=_cparams(("arbitrary",)),
        name="out_proj",
    )(y_attn, y_pool, y_gmlp, w_bf16, x, mod)


def _router_kernel(x_ref, g_ref, mod_ref, wgh_ref, wgl_ref, weh_ref, wel_ref, bg_ref, be_ref,
                   h_ref, route_ref, cnt_ref, carry_ref):
    tm = x_ref.shape[0]

    @pl.when(pl.program_id(0) == 0)
    def _():
        carry_ref[...] = jnp.zeros(carry_ref.shape, F32)

    x = x_ref[...]
    ms = jnp.mean(x * x, axis=-1, keepdims=True)
    y = x * lax.rsqrt(ms + EPS) * g_ref[...]
    h = y * (1.0 + mod_ref[0, 4:5, :]) + mod_ref[0, 3:4, :]
    h_ref[...] = h

    hh = h.astype(BF16)
    hl = (h - hh.astype(F32)).astype(BF16)

    def logits(w_hi, w_lo, b):
        return (jnp.dot(hh, w_hi[...], preferred_element_type=F32)
                + jnp.dot(hl, w_hi[...], preferred_element_type=F32)
                + jnp.dot(hh, w_lo[...], preferred_element_type=F32) + b[...])

    gl = logits(wgh_ref, wgl_ref, bg_ref)
    el = logits(weh_ref, wel_ref, be_ref)
    lane = lax.broadcasted_iota(jnp.int32, (tm, LANES), 1).astype(F32)
    far = float(LANES)

    def first_argmax(v):
        vmax = jnp.max(v, axis=-1, keepdims=True)
        idx = jnp.min(jnp.where(v == vmax, lane, far), axis=-1, keepdims=True)
        return vmax, idx

    glm = jnp.where(lane < N_GROUPS, gl, NEG_BIG)
    gmax, g_idx = first_argmax(glm)
    g_val = 1.0 / jnp.sum(jnp.exp(glm - gmax), axis=-1, keepdims=True)
    in_grp = (jnp.floor(lane * (1.0 / EXPERTS_PER_GROUP)) == g_idx) & (lane < N_EXPERTS)
    elm = jnp.where(in_grp, el, NEG_BIG)
    m1, e1 = first_argmax(elm)
    elm2 = jnp.where(lane == e1, NEG_BIG, elm)
    m2, e2 = first_argmax(elm2)
    rr = jnp.exp(m2 - m1)
    w1 = g_val / (1.0 + rr)
    w2 = g_val * rr / (1.0 + rr)

    sel1 = lane == e1
    sel2 = lane == e2
    esum = jnp.where(sel1 | sel2, 1.0, 0.0)
    rI = lax.broadcasted_iota(jnp.int32, (tm, tm), 0)
    cI = lax.broadcasted_iota(jnp.int32, (tm, tm), 1)
    ltri = jnp.where(rI > cI, 1.0, 0.0).astype(BF16)
    before = jnp.dot(ltri, esum.astype(BF16), preferred_element_type=F32) + carry_ref[0:1, :]
    rank1 = jnp.sum(jnp.where(sel1, before, 0.0), axis=-1, keepdims=True)
    rank2 = jnp.sum(jnp.where(sel2, before, 0.0), axis=-1, keepdims=True)
    carry_ref[0:1, :] = carry_ref[0:1, :] + jnp.sum(esum, axis=0, keepdims=True)
    cnt_ref[...] = carry_ref[...]

    out = jnp.where(lane == 0, e1, 0.0)
    out = jnp.where(lane == 1, e2, out)
    out = jnp.where(lane == 2, rank1, out)
    out = jnp.where(lane == 3, rank2, out)
    out = jnp.where(lane == 4, w1, out)
    out = jnp.where(lane == 5, w2, out)
    route_ref[...] = out[:, 0:8]


def _router(x, norm_g, mod, rg_w, rg_b, re_w, re_b, seg_of_tile, *, t_rows):
    d = x.shape[1]
    tm = ROW_TILE

    def pad_split(w):
        wp = jnp.zeros((d, LANES), F32).at[:, :w.shape[1]].set(w)
        hi = wp.astype(BF16)
        return hi, (wp - hi.astype(F32)).astype(BF16)

    def pad_bias(b):
        return jnp.zeros((1, LANES), F32).at[0, :b.shape[0]].set(b)

    wgh, wgl = pad_split(rg_w)
    weh, wel = pad_split(re_w)
    wspec = pl.BlockSpec((d, LANES), lambda i: (0, 0))
    bspec = pl.BlockSpec((1, LANES), lambda i: (0, 0))
    return pl.pallas_call(
        _router_kernel,
        out_shape=(jax.ShapeDtypeStruct((t_rows, d), F32),
                   jax.ShapeDtypeStruct((t_rows, 8), F32),
                   jax.ShapeDtypeStruct((8, LANES), F32)),
        grid=(t_rows // tm,),
        in_specs=[pl.BlockSpec((tm, d), lambda i: (i, 0)),
                  pl.BlockSpec((1, d), lambda i: (0, 0)),
                  pl.BlockSpec((1, 6, d), lambda i: (seg_of_tile(i, tm), 0, 0)),
                  wspec, wspec, wspec, wspec, bspec, bspec],
        out_specs=(pl.BlockSpec((tm, d), lambda i: (i, 0)),
                   pl.BlockSpec((tm, 8), lambda i: (i, 0)),
                   pl.BlockSpec((8, LANES), lambda i: (0, 0))),
        scratch_shapes=[pltpu.VMEM((8, LANES), F32)],
        compiler_params=_cparams(("arbitrary",)),
        name="moe_router",
    )(x, norm_g.reshape(1, d), mod, wgh, wgl, weh, wel, pad_bias(rg_b), pad_bias(re_b))


def _expert_kernel(blk_e_ref, n_used_ref, tok_ref, h_ref, wg_ref, wu_ref, wd_ref, ys_ref,
                   xbuf, wg_bf, wu_bf, wd_bf, sem):
    i = pl.program_id(0)
    n_used = n_used_ref[0]
    used = i < n_used
    slot = i % 2

    def row_copy(tok, buf_slot, r):
        return pltpu.make_async_copy(h_ref.at[pl.ds(tok, 1)], xbuf.at[buf_slot, pl.ds(r, 1)],
                                     sem.at[buf_slot])

    def gather(block, buf_slot):
        for r in range(MOE_BLOCK):
            row_copy(tok_ref[block * MOE_BLOCK + r], buf_slot, r).start()

    @pl.when(jnp.logical_and(i == 0, used))
    def _():
        gather(0, 0)

    prev_e = blk_e_ref[jnp.maximum(i - 1, 0)]
    new_expert = jnp.logical_or(i == 0, blk_e_ref[i] != prev_e)

    @pl.when(jnp.logical_and(used, new_expert))
    def _():
        wg_bf[...] = wg_ref[0, 0].astype(BF16)
        wu_bf[...] = wu_ref[0, 0].astype(BF16)
        wd_bf[...] = wd_ref[0, 0].astype(BF16)

    @pl.when(used)
    def _():
        for r in range(MOE_BLOCK):
            row_copy(0, slot, r).wait()

        @pl.when(i + 1 < n_used)
        def _():
            gather(i + 1, 1 - slot)

        xb = xbuf[slot].astype(BF16)
        a = jnp.dot(xb, wg_bf[...], preferred_element_type=F32)
        b = jnp.dot(xb, wu_bf[...], preferred_element_type=F32)
        hmid = (a * jax.nn.sigmoid(a) * b).astype(BF16)
        ys_ref[...] = jnp.dot(hmid, wd_bf[...], preferred_element_type=F32)

    @pl.when(jnp.logical_not(used))
    def _():
        ys_ref[...] = jnp.zeros(ys_ref.shape, F32)


def _experts(blk_e, n_used, slot_tok, h2, w_gate, w_up, w_down, layer):
    n_slots = slot_tok.shape[0]
    d = h2.shape[1]
    ff = w_gate.shape[3]
    return pl.pallas_call(
        _expert_kernel,
        out_shape=jax.ShapeDtypeStruct((n_slots, d), F32),
        grid_spec=pltpu.PrefetchScalarGridSpec(
            num_scalar_prefetch=3,
            grid=(n_slots // MOE_BLOCK,),
            in_specs=[pl.BlockSpec(memory_space=pl.ANY),
                      pl.BlockSpec((1, 1, d, ff), lambda i, be, nu, st: (layer, be[i], 0, 0)),
                      pl.BlockSpec((1, 1, d, ff), lambda i, be, nu, st: (layer, be[i], 0, 0)),
                      pl.BlockSpec((1, 1, ff, d), lambda i, be, nu, st: (layer, be[i], 0, 0))],
            out_specs=pl.BlockSpec((MOE_BLOCK, d), lambda i, be, nu, st: (i, 0)),
            scratch_shapes=[pltpu.VMEM((2, MOE_BLOCK, d), F32),
                            pltpu.VMEM((d, ff), BF16), pltpu.VMEM((d, ff), BF16),
                            pltpu.VMEM((ff, d), BF16),
                            pltpu.SemaphoreType.DMA((2,))]),
        compiler_params=_cparams(("arbitrary",)),
        name="moe_experts",
    )(blk_e, n_used, slot_tok, h2, w_gate, w_up, w_down)


def _combine_kernel(dest_ref, x_ref, mod_ref, route_ref, ys_ref, o_ref, y0_ref, y1_ref, sem):
    tm = x_ref.shape[0]
    base = pl.program_id(0) * (2 * tm)

    def row_copy(slot, buf, r):
        return pltpu.make_async_copy(ys_ref.at[pl.ds(slot, 1)], buf.at[pl.ds(r, 1)], sem)

    def issue(r, carry):
        row_copy(dest_ref[base + 2 * r], y0_ref, r).start()
        row_copy(dest_ref[base + 2 * r + 1], y1_ref, r).start()
        return carry

    def drain(r, carry):
        row_copy(0, y0_ref, 0).wait()
        row_copy(0, y1_ref, 0).wait()
        return carry

    lax.fori_loop(0, tm, issue, 0, unroll=8)
    lax.fori_loop(0, tm, drain, 0, unroll=8)
    f = route_ref[:, 4:5] * y0_ref[...] + route_ref[:, 5:6] * y1_ref[...]
    o_ref[...] = x_ref[...] + mod_ref[0, 5:6, :] * f


def _combine(dest_flat, x, mod, route, ys, seg_of_tile):
    t, d = x.shape
    tm = ROW_TILE
    return pl.pallas_call(
        _combine_kernel,
        out_shape=jax.ShapeDtypeStruct((t, d), F32),
        grid_spec=pltpu.PrefetchScalarGridSpec(
            num_scalar_prefetch=1,
            grid=(t // tm,),
            in_specs=[pl.BlockSpec((tm, d), lambda i, dest: (i, 0)),
                      pl.BlockSpec((1, 6, d), lambda i, dest: (seg_of_tile(i, tm), 0, 0)),
                      pl.BlockSpec((tm, 8), lambda i, dest: (i, 0)),
                      pl.BlockSpec(memory_space=pl.ANY)],
            out_specs=pl.BlockSpec((tm, d), lambda i, dest: (i, 0)),
            scratch_shapes=[pltpu.VMEM((tm, d), F32), pltpu.VMEM((tm, d), F32),
                            pltpu.SemaphoreType.DMA]),
        compiler_params=_cparams(("arbitrary",)),
        name="moe_combine",
    )(dest_flat, x, mod, route, ys)


def _moe(x, norm_g, mod, rg_w, rg_b, re_w, re_b, w_gate, w_up, w_down, layer, seg_of_tile):
    t = x.shape[0]
    h2, route, cnt = _router(x, norm_g, mod, rg_w, rg_b, re_w, re_b, seg_of_tile, t_rows=t)
    counts = cnt[0, :N_EXPERTS].astype(jnp.int32)
    padded = (counts + MOE_BLOCK - 1) // MOE_BLOCK * MOE_BLOCK
    pend = jnp.cumsum(padded)
    pstart = pend - padded
    n_slots = -(-(2 * t + N_EXPERTS * (MOE_BLOCK - 1)) // MOE_BLOCK) * MOE_BLOCK
    n_blk = n_slots // MOE_BLOCK
    eid = route[:, 0:2].astype(jnp.int32)
    rank = route[:, 2:4].astype(jnp.int32)
    dest = (pstart[eid] + rank).reshape(-1)
    blk_start = jnp.arange(n_blk, dtype=jnp.int32) * MOE_BLOCK
    blk_e = jnp.minimum(jnp.sum((pend[None, :] <= blk_start[:, None]).astype(jnp.int32), axis=1),
                        N_EXPERTS - 1)
    n_used = (pend[-1] // MOE_BLOCK).astype(jnp.int32).reshape(1)
    slot_tok = jnp.zeros((n_slots,), jnp.int32).at[dest].set(
        jnp.repeat(jnp.arange(t, dtype=jnp.int32), 2))
    ys = _experts(blk_e, n_used, slot_tok, h2, w_gate, w_up, w_down, layer)
    return _combine(dest, x, mod, route, ys, seg_of_tile)


def kernel(x, c, ctx, c_ctx, ada_w, ada_b, norm1_g, norm2_g, w_in, q_norm_g, k_norm_g, lam_q1, lam_k1, lam_q2, lam_k2, subln_g, pool_w, pool_scale, gmlp_norm_g, gmlp_ws, gmlp_bs, w_out, router_g_w, router_g_b, router_e_w, router_e_b, w_gate, w_up, w_down):
    batch, n, d = x.shape
    lc = ctx.shape[1]
    depth = ada_w.shape[0]
    t_lat = batch * n
    t_all = t_lat + batch * lc
    assert n % MM_ROW_TILE == 0 and (batch * lc) % MM_ROW_TILE == 0 and lc % KEY_BLOCK == 0
    assert n % Q_TILE == 0 and lc % Q_TILE == 0 and n % GRID_W == 0

    def seg_of_tile(i, tm):
        return jnp.minimum(i // (n // tm), batch)

    xs = jnp.concatenate([x.reshape(t_lat, d), ctx.reshape(batch * lc, d)], axis=0)
    c_all = jnp.concatenate([c, c_ctx[None, :]], axis=0)
    cos_t, sin_t = _rope_tables(n, KEY_BLOCK)

    for l in range(depth):
        last = l == depth - 1
        lam_init = 0.8 - 0.6 * math.exp(-0.3 * l)
        lam = (jnp.exp(jnp.sum(lam_q1[l] * lam_k1[l])) - jnp.exp(jnp.sum(lam_q2[l] * lam_k2[l]))
               + lam_init).astype(F32)
        k_bound = jnp.max(jnp.abs(k_norm_g[l])) * (math.sqrt(QK_DIM) * 1.01)
        mod = _ada(c_all, ada_w, ada_b[l], l)
        proj = _inproj(xs, norm1_g[l], mod, w_in[l].astype(BF16), seg_of_tile)
        q_all, k_all, vt_all = _prep(proj, cos_t, sin_t,
                                     jnp.tile(q_norm_g[l], 2).reshape(1, LANES),
                                     jnp.tile(k_norm_g[l], 2).reshape(1, LANES),
                                     batch=batch, n=n, lc=lc)
        t_rows = t_lat if last else t_all
        y_attn = _attention(lam, k_bound, q_all, k_all, vt_all, subln_g[l], batch=batch,
                            q_row0=0, lq=n, lk=lc + n, lam_init=lam_init)
        if not last:
            y_ctx = _attention(lam, k_bound, q_all, k_all, vt_all, subln_g[l], batch=batch,
                               q_row0=t_lat, lq=lc, lk=lc, lam_init=lam_init)
            y_attn = jnp.concatenate([y_attn, y_ctx], axis=0)
        y_pool, y_gmlp = _mixers(proj, pool_w[l].astype(BF16), pool_scale[l], gmlp_norm_g[l],
                                 gmlp_ws[l].astype(BF16), gmlp_bs[l], t_rows=t_rows,
                                 batch=batch, n=n, lc=lc)
        x1 = _outproj(y_attn, y_pool, y_gmlp, w_out[l].astype(BF16), xs, mod, seg_of_tile,
                      t_rows=t_rows)
        xs = _moe(x1, norm2_g[l], mod, router_g_w[l], router_g_b[l], router_e_w[l],
                  router_e_b[l], w_gate, w_up, w_down, l, seg_of_tile)
    return xs[:t_lat].reshape(batch, n, d)
```

```python
import functools
import math

import jax
import jax.numpy as jnp
from jax import lax
from jax.experimental import pallas as pl
from jax.experimental.pallas import tpu as pltpu

F32 = jnp.float32
BF16 = jnp.bfloat16

EPS = 1e-6
GRID_W = 64
ROPE_BASE = 10000.0
N_HEADS = 8
HEAD_DIM = 128
QK_DIM = 64
ROPE_HALF = 16
ATTN_WIDTH = N_HEADS * HEAD_DIM
POOL_WINDOWS = (2, 4, 8, 16)
POOL_GROUP = 128
POOL_WIDTH = 512
POOL_HALO = 8
GMLP_WIDTH = 512
GMLP_GROUPS = 4
GMLP_CHUNK = 128
N_GROUPS = 4
EXPERTS_PER_GROUP = 8
N_EXPERTS = 32
MOE_BLOCK = 256
GATHER_BUFS = 3
KEY_BLOCK = 256
FAST_SKEW = 2
FAST_GROUP_MAX = 65
FAST_MIN_DENOM = 2.0 ** -40
FAST_SHIFT_FRACTION = 0.2
FAST_MAX_EXPONENT = 100.0
MAX_KB_PER_TILE = 5
VT_ROWS = HEAD_DIM + 16
LANES = 128
NEG_BIG = -1e30
Q_SCALE_LOG2 = QK_DIM ** -0.5 * math.log2(math.e)

VMEM_LIMIT_V7X = 56 * 1024 * 1024

ROW_TILE = 256
MM_ROW_TILE = 512
Q_TILE = 256


def _cparams(semantics, vmem=VMEM_LIMIT_V7X, flags=None):
    return pltpu.CompilerParams(dimension_semantics=semantics, vmem_limit_bytes=vmem, flags=flags)


def _gelu_tanh(x):
    return 0.5 * x * (1.0 + jnp.tanh(math.sqrt(2.0 / math.pi) * (x + 0.044715 * x * x * x)))


def _ada_kernel(ct_ref, w_ref, b_ref, o_ref, *, nseg):
    ct = ct_ref[...]
    at = ct * jax.nn.sigmoid(ct)
    d = w_ref.shape[0]
    chunk = 256
    accs = [jnp.zeros((1, w_ref.shape[1]), F32) for _ in range(nseg)]
    for kc in range(d // chunk):
        w = w_ref[kc * chunk:(kc + 1) * chunk, :]
        for r in range(nseg):
            accs[r] = accs[r] + jnp.sum(at[kc * chunk:(kc + 1) * chunk, r:r + 1] * w,
                                        axis=0, keepdims=True)
    o_ref[...] = jnp.zeros(o_ref.shape, F32)
    for r in range(nseg):
        o_ref[r:r + 1, :] = accs[r] + b_ref[...]


def _ada(c_all, ada_w, ada_b, layer):
    nseg, d = c_all.shape
    n_out = ada_w.shape[2]
    tn = 768
    ct = jnp.zeros((d, 8), F32).at[:, :nseg].set(c_all.T)
    out = pl.pallas_call(
        functools.partial(_ada_kernel, nseg=nseg),
        out_shape=jax.ShapeDtypeStruct((8, n_out), F32),
        grid=(n_out // tn,),
        in_specs=[pl.BlockSpec((d, 8), lambda j: (0, 0)),
                  pl.BlockSpec((None, d, tn), lambda j: (layer, 0, j)),
                  pl.BlockSpec((1, tn), lambda j: (0, j))],
        out_specs=pl.BlockSpec((8, tn), lambda j: (0, j)),
        compiler_params=_cparams(("arbitrary",)),
        name="ada_mod",
    )(ct, ada_w, ada_b.reshape(1, n_out))
    return out[:nseg].reshape(nseg, 6, d)


def _inproj_kernel(x_ref, g_ref, mod_ref, w_ref, o_ref, h_ref):
    @pl.when(pl.program_id(1) == 0)
    def _():
        x = x_ref[...]
        ms = jnp.mean(x * x, axis=-1, keepdims=True)
        y = x * lax.rsqrt(ms + EPS) * g_ref[...]
        h_ref[...] = (y * (1.0 + mod_ref[0, 1:2, :]) + mod_ref[0, 0:1, :]).astype(BF16)

    o_ref[...] = jnp.dot(h_ref[...], w_ref[...], preferred_element_type=F32)


def _inproj(x, norm_g, mod, w_bf16, seg_of_tile):
    t, d = x.shape
    n_out = w_bf16.shape[1]
    tm, tn = MM_ROW_TILE, 2304
    return pl.pallas_call(
        _inproj_kernel,
        out_shape=jax.ShapeDtypeStruct((t, n_out), F32),
        grid=(t // tm, n_out // tn),
        in_specs=[pl.BlockSpec((tm, d), lambda i, j: (i, 0)),
                  pl.BlockSpec((1, d), lambda i, j: (0, 0)),
                  pl.BlockSpec((1, 6, d), lambda i, j: (seg_of_tile(i, tm), 0, 0)),
                  pl.BlockSpec((d, tn), lambda i, j: (0, j))],
        out_specs=pl.BlockSpec((tm, tn), lambda i, j: (i, j)),
        scratch_shapes=[pltpu.VMEM((tm, d), BF16)],
        compiler_params=_cparams(("arbitrary", "arbitrary")),
        name="in_proj",
    )(x, norm_g.reshape(1, d), mod, w_bf16)


def _prep_kernel(q_ref, k_ref, v_ref, cos_ref, sin_ref, qg_ref, kg_ref,
                 qo_ref, ko_ref, vo_ref):
    tm = q_ref.shape[0]
    cos = cos_ref[...]
    sin = sin_ref[...]
    lane = lax.broadcasted_iota(jnp.int32, (tm, LANES), 1)
    first_half = (lane % (2 * ROPE_HALF)) < ROPE_HALF
    r_i = lax.broadcasted_iota(jnp.int32, (LANES, LANES), 0) // QK_DIM
    c_i = lax.broadcasted_iota(jnp.int32, (LANES, LANES), 1) // QK_DIM
    group_ones = jnp.where(r_i == c_i, 1.0, 0.0).astype(BF16)

    def norm_rope(x, g, scale):
        sq = x * x
        hi = sq.astype(BF16)
        lo = (sq - hi.astype(F32)).astype(BF16)
        ss = (jnp.dot(hi, group_ones, preferred_element_type=F32)
              + jnp.dot(lo, group_ones, preferred_element_type=F32))
        y = x * lax.rsqrt(ss * (1.0 / QK_DIM) + EPS) * g
        partner = jnp.where(first_half,
                            pltpu.roll(y, LANES - ROPE_HALF, 1),
                            pltpu.roll(y, ROPE_HALF, 1))
        return (y * cos + partner * sin) * scale

    qg = qg_ref[...]
    kg = kg_ref[...]
    pad_row = lax.broadcasted_iota(jnp.int32, (VT_ROWS - HEAD_DIM, tm), 0)
    ones_rows = jnp.where(pad_row == 0, 1.0, 0.0).astype(BF16)
    for h in range(N_HEADS):
        sl = slice(h * HEAD_DIM, (h + 1) * HEAD_DIM)
        qo_ref[:, sl] = norm_rope(q_ref[:, sl], qg, Q_SCALE_LOG2).astype(BF16)
        ko_ref[0, :, sl] = norm_rope(k_ref[:, sl], kg, 1.0).astype(BF16)
        vo_ref[0, 0, h * VT_ROWS:h * VT_ROWS + HEAD_DIM, :] = v_ref[:, sl].T.astype(BF16)
        vo_ref[0, 0, h * VT_ROWS + HEAD_DIM:(h + 1) * VT_ROWS, :] = ones_rows


def _prep(proj, cos_t, sin_t, qg, kg, *, batch, n, lc):
    t = proj.shape[0]
    tm = KEY_BLOCK
    nt, nc = n // tm, lc // tm
    n_lat_tiles = batch * nt
    lk = lc + n
    nkb = lk // KEY_BLOCK

    def kv_block(i):
        lat = i < n_lat_tiles
        ic = i - n_lat_tiles
        b = jnp.where(lat, i // nt, ic // nc)
        kb = jnp.where(lat, nc + i % nt, ic % nc)
        return b, kb

    def tab_block(i):
        return jnp.where(i < n_lat_tiles, i % nt, nt)

    return pl.pallas_call(
        _prep_kernel,
        out_shape=(jax.ShapeDtypeStruct((t, ATTN_WIDTH), BF16),
                   jax.ShapeDtypeStruct((batch, lk, ATTN_WIDTH), BF16),
                   jax.ShapeDtypeStruct((batch, nkb, N_HEADS * VT_ROWS, KEY_BLOCK), BF16)),
        grid=(t // tm,),
        in_specs=[pl.BlockSpec((tm, ATTN_WIDTH), lambda i: (i, 0)),
                  pl.BlockSpec((tm, ATTN_WIDTH), lambda i: (i, 1)),
                  pl.BlockSpec((tm, ATTN_WIDTH), lambda i: (i, 2)),
                  pl.BlockSpec((tm, LANES), lambda i: (tab_block(i), 0)),
                  pl.BlockSpec((tm, LANES), lambda i: (tab_block(i), 0)),
                  pl.BlockSpec((1, LANES), lambda i: (0, 0)),
                  pl.BlockSpec((1, LANES), lambda i: (0, 0))],
        out_specs=(pl.BlockSpec((tm, ATTN_WIDTH), lambda i: (i, 0)),
                   pl.BlockSpec((1, tm, ATTN_WIDTH), lambda i: kv_block(i) + (0,)),
                   pl.BlockSpec((1, 1, N_HEADS * VT_ROWS, KEY_BLOCK),
                                lambda i: kv_block(i) + (0, 0))),
        compiler_params=_cparams(("arbitrary",)),
        name="qkv_prep",
    )(proj, proj, proj, cos_t, sin_t, qg, kg)


def _rope_tables(n, tm):
    pos = jnp.arange(n, dtype=jnp.int32)
    row = (pos // GRID_W).astype(F32)
    col = (pos % GRID_W).astype(F32)
    inv = ROPE_BASE ** (-jnp.arange(ROPE_HALF, dtype=F32) / ROPE_HALF)
    ang_r = row[:, None] * inv[None, :]
    ang_c = col[:, None] * inv[None, :]
    ang = jnp.concatenate([ang_r, ang_r, ang_c, ang_c], axis=-1)
    sign = jnp.tile(jnp.concatenate([-jnp.ones(ROPE_HALF, F32), jnp.ones(ROPE_HALF, F32)]), 2)
    cos = jnp.tile(jnp.cos(ang), (1, 2))
    sin = jnp.tile(jnp.sin(ang) * sign[None, :], (1, 2))
    cos = jnp.concatenate([cos, jnp.ones((tm, LANES), F32)], axis=0)
    sin = jnp.concatenate([sin, jnp.zeros((tm, LANES), F32)], axis=0)
    return cos, sin


NT_DIMS = (((1,), (1,)), ((), ()))


def _attn_online_pass(k_ref, vt_ref, qbd, acc_ref, m_ref, *, n_tiles, kb_per_tile):
    tk = kb_per_tile * KEY_BLOCK
    m_ref[...] = jnp.full(m_ref.shape, NEG_BIG, F32)
    acc_ref[...] = jnp.zeros(acc_ref.shape, F32)

    def tile(t, carry):
        kt = k_ref[0, pl.ds(pl.multiple_of(t * tk, tk), tk), :]
        s = lax.dot_general(kt, qbd, NT_DIMS, preferred_element_type=F32)
        m_old = m_ref[...]
        m_new = jnp.maximum(m_old, jnp.max(s, axis=0, keepdims=True))
        p = jnp.exp2((s - m_new).astype(BF16))
        vt = jnp.concatenate([vt_ref[0, t * kb_per_tile + c] for c in range(kb_per_tile)], axis=1)
        acc_ref[...] = (jnp.exp2(m_old - m_new) * acc_ref[...]
                        + jnp.dot(vt, p, preferred_element_type=F32))
        m_ref[...] = m_new
        return carry

    lax.fori_loop(0, n_tiles, tile, 0)


def _attn_kernel(par_ref, q_ref, k_ref, vt_ref, g_ref, o_ref, acc_ref, m_ref,
                 *, n_blocks, kb_per_tile, out_scale):
    tq = q_ref.shape[0]
    q = q_ref[...]
    lane = lax.broadcasted_iota(jnp.int32, (tq, HEAD_DIM), 1)
    zero = jnp.zeros_like(q)
    qbd = jnp.concatenate([jnp.where(lane < QK_DIM, q, zero),
                           jnp.where(lane >= QK_DIM, q, zero)], axis=0)

    qf = q.astype(F32)
    qsq = qf * qf
    qsq_hi = qsq.astype(BF16)
    qsq_lo = (qsq - qsq_hi.astype(F32)).astype(BF16)
    sel_r = lax.broadcasted_iota(jnp.int32, (8, HEAD_DIM), 0)
    sel_l = lax.broadcasted_iota(jnp.int32, (8, HEAD_DIM), 1)
    sel = jnp.where(sel_l // QK_DIM == sel_r, 1.0, 0.0).astype(BF16)
    qn2 = (lax.dot_general(sel, qsq_hi, NT_DIMS, preferred_element_type=F32)
           + lax.dot_general(sel, qsq_lo, NT_DIMS, preferred_element_type=F32))
    k_bound = par_ref[1]
    bound = jnp.concatenate([jnp.sqrt(qn2[0:1]) * k_bound, jnp.sqrt(qn2[1:2]) * k_bound],
                            axis=1)
    shift = jnp.maximum(FAST_SHIFT_FRACTION * bound, bound - FAST_MAX_EXPONENT)
    acc_ref[...] = jnp.zeros(acc_ref.shape, F32)
    group = max(c for c in range(1, FAST_GROUP_MAX + 1) if n_blocks % c == 0)

    def fast_group(u, carry):
        pv = None
        scores = {}
        for cc in range(group + FAST_SKEW):
            if cc < group:
                row0 = pl.multiple_of((u * group + cc) * KEY_BLOCK, KEY_BLOCK)
                kt = k_ref[0, pl.ds(row0, KEY_BLOCK), :]
                scores[cc] = lax.dot_general(kt, qbd, NT_DIMS, preferred_element_type=F32)
            c = cc - FAST_SKEW
            if c >= 0:
                p = jnp.exp2((scores.pop(c) - shift).astype(BF16))
                d = jnp.dot(vt_ref[0, u * group + c], p, preferred_element_type=F32)
                pv = d if pv is None else pv + d
        acc_ref[...] += pv
        return carry

    if n_blocks == group:
        fast_group(0, 0)
    else:
        lax.fori_loop(0, n_blocks // group, fast_group, 0)
    denom_min = jnp.min(acc_ref[HEAD_DIM:HEAD_DIM + 1, :])

    @pl.when(jnp.logical_not(denom_min >= FAST_MIN_DENOM))
    def _():
        _attn_online_pass(k_ref, vt_ref, qbd, acc_ref, m_ref,
                          n_tiles=n_blocks // kb_per_tile, kb_per_tile=kb_per_tile)

    o = acc_ref[0:HEAD_DIM, :] / acc_ref[HEAD_DIM:HEAD_DIM + 1, :]
    d = o[:, :tq] - par_ref[0] * o[:, tq:]
    ms = jnp.mean(d * d, axis=0, keepdims=True)
    y = d * lax.rsqrt(ms + EPS) * (g_ref[...] * out_scale)
    o_ref[...] = y.T.astype(BF16)


def _attention(lam, k_bound, q_all, k_all, vt_all, subln_g, *, batch, q_row0, lq, lk, lam_init):
    tq = Q_TILE
    nq = lq // tq
    nkb = lk // KEY_BLOCK
    kb_per_tile = max(c for c in range(1, MAX_KB_PER_TILE + 1) if nkb % c == 0)
    qb0 = q_row0 // tq
    kernel = functools.partial(_attn_kernel, n_blocks=nkb, kb_per_tile=kb_per_tile,
                               out_scale=1.0 - lam_init)
    return pl.pallas_call(
        kernel,
        out_shape=jax.ShapeDtypeStruct((batch * lq, ATTN_WIDTH), BF16),
        grid=(batch, N_HEADS, nq),
        in_specs=[pl.BlockSpec(memory_space=pltpu.SMEM),
                  pl.BlockSpec((tq, HEAD_DIM), lambda b, h, qi: (qb0 + b * nq + qi, h)),
                  pl.BlockSpec((1, lk, HEAD_DIM), lambda b, h, qi: (b, 0, h)),
                  pl.BlockSpec((1, nkb, VT_ROWS, KEY_BLOCK), lambda b, h, qi: (b, 0, h, 0)),
                  pl.BlockSpec((HEAD_DIM, 1), lambda b, h, qi: (0, 0))],
        out_specs=pl.BlockSpec((tq, HEAD_DIM), lambda b, h, qi: (b * nq + qi, h)),
        scratch_shapes=[pltpu.VMEM((VT_ROWS, 2 * tq), F32),
                        pltpu.VMEM((1, 2 * tq), F32)],
        compiler_params=_cparams(("arbitrary", "arbitrary", "arbitrary")),
        name="diff_attn",
    )(jnp.stack([lam, k_bound]).astype(F32), q_all, k_all, vt_all,
      subln_g.reshape(HEAD_DIM, 1))


def _mixers_kernel(p_ref, pprev_ref, pnext_ref, u_ref, v_ref, pw_ref, ps_ref,
                   ng_ref, ws_ref, bst_ref, yp_ref, yg_ref, *, tiles_per_seq):
    tm = p_ref.shape[0]
    i = pl.program_id(0)
    nt, n_lat, nc = tiles_per_seq
    lat = i < n_lat
    pos = jnp.where(lat, i % nt, (i - n_lat) % nc)
    per = jnp.where(lat, nt, nc)
    first = pos == 0
    last = pos == per - 1

    r = lax.broadcasted_iota(jnp.int32, (tm, tm), 0)
    u = lax.broadcasted_iota(jnp.int32, (tm, tm), 1)
    rh = lax.broadcasted_iota(jnp.int32, (tm, 2 * POOL_HALO), 0)
    uh = lax.broadcasted_iota(jnp.int32, (tm, 2 * POOL_HALO), 1)
    uh_pos = jnp.where(uh < POOL_HALO, uh - POOL_HALO, tm + uh - POOL_HALO)
    uh_ok = jnp.where(uh < POOL_HALO, 1 - first.astype(jnp.int32), 1 - last.astype(jnp.int32)) > 0
    rcol = lax.broadcasted_iota(jnp.int32, (tm, 1), 0)
    seq_lo = jnp.where(first, 0, -POOL_HALO)
    seq_hi = jnp.where(last, tm - 1, tm - 1 + POOL_HALO)

    p = p_ref[...]
    halo = jnp.concatenate([pprev_ref[...], pnext_ref[...]], axis=0)

    def split(x):
        hi = x.astype(BF16)
        return hi, (x - hi.astype(F32)).astype(BF16)

    p_hi, p_lo = split(p)
    h_hi, h_lo = split(halo)
    for gi, win in enumerate(POOL_WINDOWS):
        sl = slice(gi * POOL_GROUP, (gi + 1) * POOL_GROUP)
        lo_off, hi_off = win // 2, win - 1 - win // 2
        band_c = jnp.where((u >= r - lo_off) & (u <= r + hi_off), 1.0, 0.0).astype(BF16)
        band_h = jnp.where((uh_pos >= rh - lo_off) & (uh_pos <= rh + hi_off) & uh_ok,
                           1.0, 0.0).astype(BF16)
        tot = (jnp.dot(band_c, p_hi[:, sl], preferred_element_type=F32)
               + jnp.dot(band_c, p_lo[:, sl], preferred_element_type=F32)
               + jnp.dot(band_h, h_hi[:, sl], preferred_element_type=F32)
               + jnp.dot(band_h, h_lo[:, sl], preferred_element_type=F32))
        lo = jnp.maximum(rcol - lo_off, seq_lo)
        hi = jnp.minimum(rcol + hi_off, seq_hi)
        cnt = (hi - lo + 1).astype(F32)
        dlt = tot / cnt - p[:, sl]
        y = jnp.dot(dlt.astype(BF16), pw_ref[gi], preferred_element_type=F32)
        yp_ref[:, sl] = (y * ps_ref[:, sl]).astype(BF16)

    uu = _gelu_tanh(u_ref[...])
    vv = _gelu_tanh(v_ref[...])
    mu = jnp.mean(vv, axis=-1, keepdims=True)
    vc = vv - mu
    vn = (vc * lax.rsqrt(jnp.mean(vc * vc, axis=-1, keepdims=True) + EPS) * ng_ref[...]).astype(BF16)
    for c in range(tm // GMLP_CHUNK):
        rows = slice(c * GMLP_CHUNK, (c + 1) * GMLP_CHUNK)
        for g in range(GMLP_GROUPS):
            sl = slice(g * GMLP_CHUNK, (g + 1) * GMLP_CHUNK)
            sv = jnp.dot(ws_ref[g], vn[rows, sl], preferred_element_type=F32) + bst_ref[:, g:g + 1]
            yg_ref[rows, sl] = (uu[rows, sl] * sv).astype(BF16)


def _mixers(proj, pool_w_bf16, pool_scale, gmlp_norm_g, gmlp_ws_bf16, gmlp_bs, *, t_rows,
            batch, n, lc):
    tm = ROW_TILE
    nt, nc = n // tm, lc // tm
    n_lat = batch * nt
    hb = tm // POOL_HALO
    n_hblocks = proj.shape[0] // POOL_HALO
    pool_cb, gu_cb, gv_cb = 3 * ATTN_WIDTH // POOL_WIDTH, 3 * ATTN_WIDTH // POOL_WIDTH + 1, \
        3 * ATTN_WIDTH // POOL_WIDTH + 2
    kernel = functools.partial(_mixers_kernel, tiles_per_seq=(nt, n_lat, nc))
    return pl.pallas_call(
        kernel,
        out_shape=(jax.ShapeDtypeStruct((t_rows, POOL_WIDTH), BF16),
                   jax.ShapeDtypeStruct((t_rows, GMLP_WIDTH), BF16)),
        grid=(t_rows // tm,),
        in_specs=[pl.BlockSpec((tm, POOL_WIDTH), lambda i: (i, pool_cb)),
                  pl.BlockSpec((POOL_HALO, POOL_WIDTH),
                               lambda i: (jnp.maximum(i * hb - 1, 0), pool_cb)),
                  pl.BlockSpec((POOL_HALO, POOL_WIDTH),
                               lambda i: (jnp.minimum((i + 1) * hb, n_hblocks - 1), pool_cb)),
                  pl.BlockSpec((tm, GMLP_WIDTH), lambda i: (i, gu_cb)),
                  pl.BlockSpec((tm, GMLP_WIDTH), lambda i: (i, gv_cb)),
                  pl.BlockSpec((len(POOL_WINDOWS), POOL_GROUP, POOL_GROUP), lambda i: (0, 0, 0)),
                  pl.BlockSpec((1, POOL_WIDTH), lambda i: (0, 0)),
                  pl.BlockSpec((1, GMLP_WIDTH), lambda i: (0, 0)),
                  pl.BlockSpec((GMLP_GROUPS, GMLP_CHUNK, GMLP_CHUNK), lambda i: (0, 0, 0)),
                  pl.BlockSpec((GMLP_CHUNK, GMLP_GROUPS), lambda i: (0, 0))],
        out_specs=(pl.BlockSpec((tm, POOL_WIDTH), lambda i: (i, 0)),
                   pl.BlockSpec((tm, GMLP_WIDTH), lambda i: (i, 0))),
        compiler_params=_cparams(("arbitrary",)),
        name="local_mixers",
    )(proj, proj, proj, proj, proj, pool_w_bf16, pool_scale.reshape(1, POOL_WIDTH),
      gmlp_norm_g.reshape(1, GMLP_WIDTH), gmlp_ws_bf16, gmlp_bs.T)


def _outproj_kernel(ya_ref, yp_ref, yg_ref, w_ref, x_ref, mod_ref, o_ref):
    acc = jnp.dot(ya_ref[...], w_ref[0:ATTN_WIDTH, :], preferred_element_type=F32)
    acc = acc + jnp.dot(yp_ref[...], w_ref[ATTN_WIDTH:ATTN_WIDTH + POOL_WIDTH, :],
                        preferred_element_type=F32)
    acc = acc + jnp.dot(yg_ref[...], w_ref[ATTN_WIDTH + POOL_WIDTH:, :],
                        preferred_element_type=F32)
    o_ref[...] = x_ref[...] + mod_ref[0, 2:3, :] * acc


def _outproj(y_attn, y_pool, y_gmlp, w_bf16, x, mod, seg_of_tile, *, t_rows):
    d = x.shape[1]
    tm = MM_ROW_TILE
    return pl.pallas_call(
        _outproj_kernel,
        out_shape=jax.ShapeDtypeStruct((t_rows, d), F32),
        grid=(t_rows // tm,),
        in_specs=[pl.BlockSpec((tm, ATTN_WIDTH), lambda i: (i, 0)),
                  pl.BlockSpec((tm, POOL_WIDTH), lambda i: (i, 0)),
                  pl.BlockSpec((tm, GMLP_WIDTH), lambda i: (i, 0)),
                  pl.BlockSpec(w_bf16.shape, lambda i: (0, 0)),
                  pl.BlockSpec((tm, d), lambda i: (i, 0)),
                  pl.BlockSpec((1, 6, d), lambda i: (seg_of_tile(i, tm), 0, 0))],
        out_specs=pl.BlockSpec((tm, d), lambda i: (i, 0)),
        compiler_params=_cparams(("arbitrary",)),
        name="out_proj",
    )(y_attn, y_pool, y_gmlp, w_bf16, x, mod)


def _router_kernel(x_ref, g_ref, mod_ref, wgh_ref, wgl_ref, weh_ref, wel_ref, bg_ref, be_ref,
                   h_ref, route_ref, cnt_ref, carry_ref):
    tm = x_ref.shape[0]

    @pl.when(pl.program_id(0) == 0)
    def _():
        carry_ref[...] = jnp.zeros(carry_ref.shape, F32)

    x = x_ref[...]
    ms = jnp.mean(x * x, axis=-1, keepdims=True)
    y = x * lax.rsqrt(ms + EPS) * g_ref[...]
    h = y * (1.0 + mod_ref[0, 4:5, :]) + mod_ref[0, 3:4, :]
    h_ref[...] = h

    hh = h.astype(BF16)
    hl = (h - hh.astype(F32)).astype(BF16)

    def logits(w_hi, w_lo, b):
        return (jnp.dot(hh, w_hi[...], preferred_element_type=F32)
                + jnp.dot(hl, w_hi[...], preferred_element_type=F32)
                + jnp.dot(hh, w_lo[...], preferred_element_type=F32) + b[...])

    gl = logits(wgh_ref, wgl_ref, bg_ref)
    el = logits(weh_ref, wel_ref, be_ref)
    lane = lax.broadcasted_iota(jnp.int32, (tm, LANES), 1).astype(F32)
    far = float(LANES)

    def first_argmax(v):
        vmax = jnp.max(v, axis=-1, keepdims=True)
        idx = jnp.min(jnp.where(v == vmax, lane, far), axis=-1, keepdims=True)
        return vmax, idx

    glm = jnp.where(lane < N_GROUPS, gl, NEG_BIG)
    gmax, g_idx = first_argmax(glm)
    g_val = 1.0 / jnp.sum(jnp.exp(glm - gmax), axis=-1, keepdims=True)
    in_grp = (jnp.floor(lane * (1.0 / EXPERTS_PER_GROUP)) == g_idx) & (lane < N_EXPERTS)
    elm = jnp.where(in_grp, el, NEG_BIG)
    m1, e1 = first_argmax(elm)
    elm2 = jnp.where(lane == e1, NEG_BIG, elm)
    m2, e2 = first_argmax(elm2)
    rr = jnp.exp(m2 - m1)
    w1 = g_val / (1.0 + rr)
    w2 = g_val * rr / (1.0 + rr)

    sel1 = lane == e1
    sel2 = lane == e2
    esum = jnp.where(sel1 | sel2, 1.0, 0.0)
    rI = lax.broadcasted_iota(jnp.int32, (tm, tm), 0)
    cI = lax.broadcasted_iota(jnp.int32, (tm, tm), 1)
    ltri = jnp.where(rI > cI, 1.0, 0.0).astype(BF16)
    before = jnp.dot(ltri, esum.astype(BF16), preferred_element_type=F32) + carry_ref[0:1, :]
    rank1 = jnp.sum(jnp.where(sel1, before, 0.0), axis=-1, keepdims=True)
    rank2 = jnp.sum(jnp.where(sel2, before, 0.0), axis=-1, keepdims=True)
    carry_ref[0:1, :] = carry_ref[0:1, :] + jnp.sum(esum, axis=0, keepdims=True)
    cnt_ref[...] = carry_ref[...]

    out = jnp.where(lane == 0, e1, 0.0)
    out = jnp.where(lane == 1, e2, out)
    out = jnp.where(lane == 2, rank1, out)
    out = jnp.where(lane == 3, rank2, out)
    out = jnp.where(lane == 4, w1, out)
    out = jnp.where(lane == 5, w2, out)
    route_ref[...] = out[:, 0:8]


def _router(x, norm_g, mod, rg_w, rg_b, re_w, re_b, seg_of_tile, *, t_rows):
    d = x.shape[1]
    tm = ROW_TILE

    def pad_split(w):
        wp = jnp.zeros((d, LANES), F32).at[:, :w.shape[1]].set(w)
        hi = wp.astype(BF16)
        return hi, (wp - hi.astype(F32)).astype(BF16)

    def pad_bias(b):
        return jnp.zeros((1, LANES), F32).at[0, :b.shape[0]].set(b)

    wgh, wgl = pad_split(rg_w)
    weh, wel = pad_split(re_w)
    wspec = pl.BlockSpec((d, LANES), lambda i: (0, 0))
    bspec = pl.BlockSpec((1, LANES), lambda i: (0, 0))
    return pl.pallas_call(
        _router_kernel,
        out_shape=(jax.ShapeDtypeStruct((t_rows, d), F32),
                   jax.ShapeDtypeStruct((t_rows, 8), F32),
                   jax.ShapeDtypeStruct((8, LANES), F32)),
        grid=(t_rows // tm,),
        in_specs=[pl.BlockSpec((tm, d), lambda i: (i, 0)),
                  pl.BlockSpec((1, d), lambda i: (0, 0)),
                  pl.BlockSpec((1, 6, d), lambda i: (seg_of_tile(i, tm), 0, 0)),
                  wspec, wspec, wspec, wspec, bspec, bspec],
        out_specs=(pl.BlockSpec((tm, d), lambda i: (i, 0)),
                   pl.BlockSpec((tm, 8), lambda i: (i, 0)),
                   pl.BlockSpec((8, LANES), lambda i: (0, 0))),
        scratch_shapes=[pltpu.VMEM((8, LANES), F32)],
        compiler_params=_cparams(("arbitrary",)),
        name="moe_router",
    )(x, norm_g.reshape(1, d), mod, wgh, wgl, weh, wel, pad_bias(rg_b), pad_bias(re_b))


def _expert_kernel(blk_e_ref, n_used_ref, tok_ref, h_ref, wg_ref, wu_ref, wd_ref, ys_ref,
                   xbuf, wg_bf, wu_bf, wd_bf, sem):
    i = pl.program_id(0)
    n_used = n_used_ref[0]
    used = i < n_used
    slot = i % GATHER_BUFS
    ahead = GATHER_BUFS - 1

    def row_copy(tok, buf_slot, r):
        return pltpu.make_async_copy(h_ref.at[pl.ds(tok, 1)], xbuf.at[buf_slot, pl.ds(r, 1)],
                                     sem.at[buf_slot])

    def gather(block, buf_slot):
        for r in range(MOE_BLOCK):
            row_copy(tok_ref[block * MOE_BLOCK + r], buf_slot, r).start()

    for b0 in range(ahead):
        @pl.when(jnp.logical_and(i == 0, b0 < n_used))
        def _(b0=b0):
            gather(b0, b0)

    prev_e = blk_e_ref[jnp.maximum(i - 1, 0)]
    new_expert = jnp.logical_or(i == 0, blk_e_ref[i] != prev_e)

    @pl.when(jnp.logical_and(used, new_expert))
    def _():
        wg_bf[...] = wg_ref[0, 0].astype(BF16)
        wu_bf[...] = wu_ref[0, 0].astype(BF16)
        wd_bf[...] = wd_ref[0, 0].astype(BF16)

    @pl.when(used)
    def _():
        for r in range(MOE_BLOCK):
            row_copy(0, slot, r).wait()

        @pl.when(i + ahead < n_used)
        def _():
            gather(i + ahead, (i + ahead) % GATHER_BUFS)

        xb = xbuf[slot].astype(BF16)
        a = jnp.dot(xb, wg_bf[...], preferred_element_type=F32)
        b = jnp.dot(xb, wu_bf[...], preferred_element_type=F32)
        hmid = (a * jax.nn.sigmoid(a) * b).astype(BF16)
        ys_ref[...] = jnp.dot(hmid, wd_bf[...], preferred_element_type=F32)

    @pl.when(jnp.logical_not(used))
    def _():
        ys_ref[...] = jnp.zeros(ys_ref.shape, F32)


def _experts(blk_e, n_used, slot_tok, h2, w_gate, w_up, w_down, layer):
    n_slots = slot_tok.shape[0]
    d = h2.shape[1]
    ff = w_gate.shape[3]
    return pl.pallas_call(
        _expert_kernel,
        out_shape=jax.ShapeDtypeStruct((n_slots, d), F32),
        grid_spec=pltpu.PrefetchScalarGridSpec(
            num_scalar_prefetch=3,
            grid=(n_slots // MOE_BLOCK,),
            in_specs=[pl.BlockSpec(memory_space=pl.ANY),
                      pl.BlockSpec((1, 1, d, ff), lambda i, be, nu, st: (layer, be[i], 0, 0)),
                      pl.BlockSpec((1, 1, d, ff), lambda i, be, nu, st: (layer, be[i], 0, 0)),
                      pl.BlockSpec((1, 1, ff, d), lambda i, be, nu, st: (layer, be[i], 0, 0))],
            out_specs=pl.BlockSpec((MOE_BLOCK, d), lambda i, be, nu, st: (i, 0)),
            scratch_shapes=[pltpu.VMEM((GATHER_BUFS, MOE_BLOCK, d), F32),
                            pltpu.VMEM((d, ff), BF16), pltpu.VMEM((d, ff), BF16),
                            pltpu.VMEM((ff, d), BF16),
                            pltpu.SemaphoreType.DMA((GATHER_BUFS,))]),
        compiler_params=_cparams(("arbitrary",)),
        name="moe_experts",
    )(blk_e, n_used, slot_tok, h2, w_gate, w_up, w_down)


def _combine_kernel(dest_ref, x_ref, mod_ref, route_ref, ys_ref, o_ref, y0_ref, y1_ref, sem):
    tm = x_ref.shape[0]
    base = pl.program_id(0) * (2 * tm)

    def row_copy(slot, buf, r):
        return pltpu.make_async_copy(ys_ref.at[pl.ds(slot, 1)], buf.at[pl.ds(r, 1)], sem)

    def issue(r, carry):
        row_copy(dest_ref[base + 2 * r], y0_ref, r).start()
        row_copy(dest_ref[base + 2 * r + 1], y1_ref, r).start()
        return carry

    def drain(r, carry):
        row_copy(0, y0_ref, 0).wait()
        row_copy(0, y1_ref, 0).wait()
        return carry

    lax.fori_loop(0, tm, issue, 0, unroll=8)
    lax.fori_loop(0, tm, drain, 0, unroll=8)
    f = route_ref[:, 4:5] * y0_ref[...] + route_ref[:, 5:6] * y1_ref[...]
    o_ref[...] = x_ref[...] + mod_ref[0, 5:6, :] * f


def _combine(dest_flat, x, mod, route, ys, seg_of_tile):
    t, d = x.shape
    tm = ROW_TILE
    return pl.pallas_call(
        _combine_kernel,
        out_shape=jax.ShapeDtypeStruct((t, d), F32),
        grid_spec=pltpu.PrefetchScalarGridSpec(
            num_scalar_prefetch=1,
            grid=(t // tm,),
            in_specs=[pl.BlockSpec((tm, d), lambda i, dest: (i, 0)),
                      pl.BlockSpec((1, 6, d), lambda i, dest: (seg_of_tile(i, tm), 0, 0)),
                      pl.BlockSpec((tm, 8), lambda i, dest: (i, 0)),
                      pl.BlockSpec(memory_space=pl.ANY)],
            out_specs=pl.BlockSpec((tm, d), lambda i, dest: (i, 0)),
            scratch_shapes=[pltpu.VMEM((tm, d), F32), pltpu.VMEM((tm, d), F32),
                            pltpu.SemaphoreType.DMA]),
        compiler_params=_cparams(("arbitrary",)),
        name="moe_combine",
    )(dest_flat, x, mod, route, ys)


def _moe(x, norm_g, mod, rg_w, rg_b, re_w, re_b, w_gate, w_up, w_down, layer, seg_of_tile):
    t = x.shape[0]
    h2, route, cnt = _router(x, norm_g, mod, rg_w, rg_b, re_w, re_b, seg_of_tile, t_rows=t)
    counts = cnt[0, :N_EXPERTS].astype(jnp.int32)
    padded = (counts + MOE_BLOCK - 1) // MOE_BLOCK * MOE_BLOCK
    pend = jnp.cumsum(padded)
    pstart = pend - padded
    n_slots = -(-(2 * t + N_EXPERTS * (MOE_BLOCK - 1)) // MOE_BLOCK) * MOE_BLOCK
    n_blk = n_slots // MOE_BLOCK
    eid = route[:, 0:2].astype(jnp.int32)
    rank = route[:, 2:4].astype(jnp.int32)
    dest = (pstart[eid] + rank).reshape(-1)
    blk_start = jnp.arange(n_blk, dtype=jnp.int32) * MOE_BLOCK
    blk_e = jnp.minimum(jnp.sum((pend[None, :] <= blk_start[:, None]).astype(jnp.int32), axis=1),
                        N_EXPERTS - 1)
    n_used = (pend[-1] // MOE_BLOCK).astype(jnp.int32).reshape(1)
    slot_tok = jnp.zeros((n_slots,), jnp.int32).at[dest].set(
        jnp.repeat(jnp.arange(t, dtype=jnp.int32), 2))
    ys = _experts(blk_e, n_used, slot_tok, h2, w_gate, w_up, w_down, layer)
    return _combine(dest, x, mod, route, ys, seg_of_tile)


def kernel(x, c, ctx, c_ctx, ada_w, ada_b, norm1_g, norm2_g, w_in, q_norm_g, k_norm_g, lam_q1, lam_k1, lam_q2, lam_k2, subln_g, pool_w, pool_scale, gmlp_norm_g, gmlp_ws, gmlp_bs, w_out, router_g_w, router_g_b, router_e_w, router_e_b, w_gate, w_up, w_down):
    batch, n, d = x.shape
    lc = ctx.shape[1]
    depth = ada_w.shape[0]
    t_lat = batch * n
    t_all = t_lat + batch * lc
    assert n % MM_ROW_TILE == 0 and (batch * lc) % MM_ROW_TILE == 0 and lc % KEY_BLOCK == 0
    assert n % Q_TILE == 0 and lc % Q_TILE == 0 and n % GRID_W == 0

    def seg_of_tile(i, tm):
        return jnp.minimum(i // (n // tm), batch)

    xs = jnp.concatenate([x.reshape(t_lat, d), ctx.reshape(batch * lc, d)], axis=0)
    c_all = jnp.concatenate([c, c_ctx[None, :]], axis=0)
    cos_t, sin_t = _rope_tables(n, KEY_BLOCK)

    for l in range(depth):
        last = l == depth - 1
        lam_init = 0.8 - 0.6 * math.exp(-0.3 * l)
        lam = (jnp.exp(jnp.sum(lam_q1[l] * lam_k1[l])) - jnp.exp(jnp.sum(lam_q2[l] * lam_k2[l]))
               + lam_init).astype(F32)
        k_bound = jnp.max(jnp.abs(k_norm_g[l])) * (math.sqrt(QK_DIM) * 1.01)
        mod = _ada(c_all, ada_w, ada_b[l], l)
        proj = _inproj(xs, norm1_g[l], mod, w_in[l].astype(BF16), seg_of_tile)
        q_all, k_all, vt_all = _prep(proj, cos_t, sin_t,
                                     jnp.tile(q_norm_g[l], 2).reshape(1, LANES),
                                     jnp.tile(k_norm_g[l], 2).reshape(1, LANES),
                                     batch=batch, n=n, lc=lc)
        t_rows = t_lat if last else t_all
        y_attn = _attention(lam, k_bound, q_all, k_all, vt_all, subln_g[l], batch=batch,
                            q_row0=0, lq=n, lk=lc + n, lam_init=lam_init)
        if not last:
            y_ctx = _attention(lam, k_bound, q_all, k_all, vt_all, subln_g[l], batch=batch,
                               q_row0=t_lat, lq=lc, lk=lc, lam_init=lam_init)
            y_attn = jnp.concatenate([y_attn, y_ctx], axis=0)
        y_pool, y_gmlp = _mixers(proj, pool_w[l].astype(BF16), pool_scale[l], gmlp_norm_g[l],
                                 gmlp_ws[l].astype(BF16), gmlp_bs[l], t_rows=t_rows,
                                 batch=batch, n=n, lc=lc)
        x1 = _outproj(y_attn, y_pool, y_gmlp, w_out[l].astype(BF16), xs, mod, seg_of_tile,
                      t_rows=t_rows)
        xs = _moe(x1, norm2_g[l], mod, router_g_w[l], router_g_b[l], router_e_w[l],
                  router_e_b[l], w_gate, w_up, w_down, l, seg_of_tile)
    return xs[:t_lat].reshape(batch, n, d)
```
